```python
import math
import jax, jax.numpy as jnp
from jax import lax
import numpy as np

D_MODEL = 4096
BATCH = 4
SEQ = 4096
DEPTH = 2

HEAD_DIM = 128
N_HEADS = D_MODEL // HEAD_DIM
DILATED_PATTERNS = ((128, 1), (512, 4), (2048, 16))
N_DIL = len(DILATED_PATTERNS)
BLOCK = 128
NUM_BUCKETS = 32
MAX_DISTANCE = 2048
POOL_WINDOWS = (2, 4, 8, 16)
POOL_GROUP = D_MODEL // len(POOL_WINDOWS)
D_FF_DENSE = 11008 * D_MODEL // 4096
D_FF_EXPERT = 3584 * D_MODEL // 4096
N_EXPERTS = 8
TOP_K = 2
N_A = DEPTH // 2
N_B = DEPTH - N_A
N_DENSE = (DEPTH + 1) // 2
N_MOE = DEPTH // 2
EPS = 1e-6
NEG_INF = -1e30

kernel_name = 'hybrid_pool_dilated_moe_yoco'


def rmsnorm(x, g):
    x32 = x.astype(jnp.float32)
    y = x32 * lax.rsqrt(jnp.mean(x32 * x32, axis=-1, keepdims=True) + EPS)
    return y.astype(x.dtype) * g


def head_rms(t, g):
    t32 = t.astype(jnp.float32)
    y = t32 * lax.rsqrt(jnp.mean(t32 * t32, axis=-1, keepdims=True) + EPS)
    return y.astype(t.dtype) * g


def modulate(h, shift, scale):
    return h * (1 + scale[:, None, :]) + shift[:, None, :]


def t5_bucket(n):
    max_exact = NUM_BUCKETS // 2
    nf = jnp.maximum(n, 1).astype(jnp.float32)
    large = max_exact + (jnp.log(nf / max_exact) / math.log(MAX_DISTANCE / max_exact)
                         * (NUM_BUCKETS - max_exact)).astype(jnp.int32)
    return jnp.where(n < max_exact, n, jnp.minimum(large, NUM_BUCKETS - 1))


def band_geometry(window, dilation):
    n_taps = window // dilation
    i = jnp.arange(BLOCK)[:, None]
    kk = jnp.arange(2 * BLOCK)[None, :]
    rel = i + BLOCK - kk
    valid = (rel >= 0) & (rel <= n_taps)
    bucket = t5_bucket(jnp.maximum(rel, 0) * dilation)
    return valid, bucket


def pool_mixer(h, w, scale):
    B, S, D = h.shape
    h32 = h.astype(jnp.float32)
    cs = jnp.cumsum(h32, axis=1)
    t = jnp.arange(S)
    parts = []
    for g, k in enumerate(POOL_WINDOWS):
        sl = slice(g * POOL_GROUP, (g + 1) * POOL_GROUP)
        c_g = cs[..., sl]
        lagged = jnp.pad(c_g, ((0, 0), (k, 0), (0, 0)))[:, :S]
        cnt = jnp.minimum(t + 1, k).astype(jnp.float32)[None, :, None]
        parts.append((c_g - lagged) / cnt - h32[..., sl])
    p = jnp.stack(parts, axis=2).astype(h.dtype)
    y = jnp.einsum('bsgc,gcd->bsgd', p, w).reshape(B, S, D)
    return y * scale


def dilated_branch(q, k, v, bias, valid, dilation):
    B, S, H, E = q.shape
    L = S // dilation
    nb = -(-L // BLOCK)
    Lp = nb * BLOCK

    def to_blocks(t):
        t = t.reshape(B, L, dilation, H, E).transpose(0, 2, 1, 3, 4)
        t = jnp.pad(t, ((0, 0), (0, 0), (0, Lp - L), (0, 0), (0, 0)))
        return t.reshape(B, dilation, nb, BLOCK, H, E)

    def band(t):
        prev = jnp.pad(t, ((0, 0), (0, 0), (1, 0), (0, 0), (0, 0), (0, 0)))[:, :, :nb]
        return jnp.concatenate([prev, t], axis=3)

    qb = to_blocks(q)
    kb = band(to_blocks(k))
    vb = band(to_blocks(v))
    s = jnp.einsum('brnqhe,brnkhe->brnhqk', qb, kb,
                   preferred_element_type=jnp.float32) * (E ** -0.5) + bias
    key_idx = jnp.arange(nb)[:, None] * BLOCK - BLOCK + jnp.arange(2 * BLOCK)[None, :]
    mask = valid[None] & (key_idx >= 0)[:, None, :]
    s = jnp.where(mask[None, None, :, None], s, NEG_INF)
    m = jnp.max(s, axis=-1, keepdims=True)
    p = jnp.exp(s - m)
    l = jnp.sum(p, axis=-1)
    o = jnp.einsum('brnhqk,brnkhe->brnqhe', p.astype(v.dtype), vb,
                   preferred_element_type=jnp.float32)
    o = o / jnp.transpose(l, (0, 1, 2, 4, 3))[..., None]
    lse = jnp.transpose(m[..., 0] + jnp.log(l), (0, 1, 2, 4, 3))
    o = o.reshape(B, dilation, Lp, H, E)[:, :, :L].transpose(0, 2, 1, 3, 4).reshape(B, S, H, E)
    lse = lse.reshape(B, dilation, Lp, H)[:, :, :L].transpose(0, 2, 1, 3).reshape(B, S, H)
    return o, lse


def dilated_attention(h, k, v, w_q, q_g, w_o, rel_bias):
    B, S, _ = h.shape
    q = head_rms((h @ w_q).reshape(B, S, N_DIL, N_HEADS, HEAD_DIM), q_g)
    outs, lses = [], []
    for g, (window, dilation) in enumerate(DILATED_PATTERNS):
        valid, bucket = band_geometry(window, dilation)
        bias = jnp.transpose(rel_bias[bucket, g], (2, 0, 1)).astype(jnp.float32)
        o, lse = dilated_branch(q[:, :, g], k, v, bias, valid, dilation)
        outs.append(o)
        lses.append(lse)
    wts = jax.nn.softmax(jnp.stack(lses), axis=0)
    o = jnp.einsum('gbsh,gbshe->bshe', wts, jnp.stack(outs))
    return o.reshape(B, S, N_HEADS * HEAD_DIM).astype(h.dtype) @ w_o


def swiglu(h, w1, w3, w2):
    return (jax.nn.silu(h @ w1) * (h @ w3)) @ w2


def moe_swiglu(h, router_w, w1, w3, w2):
    B, S, D = h.shape
    t = h.reshape(B * S, D)
    logits = (t @ router_w).astype(jnp.float32)
    top_v, top_i = lax.top_k(logits, TOP_K)
    gates = jax.nn.softmax(top_v, axis=-1)
    combine = jnp.sum(jax.nn.one_hot(top_i, N_EXPERTS, dtype=jnp.float32) * gates[..., None], axis=1)
    combine = combine.astype(t.dtype)
    out = jnp.zeros_like(t)
    for e in range(N_EXPERTS):
        out = out + combine[:, e:e + 1] * swiglu(t, w1[e], w3[e], w2[e])
    return out.reshape(B, S, D)


def setup_inputs(seed: int = 0) -> dict:
    key = jax.random.key(seed)
    ks = jax.random.split(key, 24)
    f32 = jnp.float32
    D, H, E = D_MODEL, N_HEADS, HEAD_DIM
    nrm = lambda k, shape, s: jax.random.normal(k, shape, f32) * s
    return {
        'x': nrm(ks[0], (BATCH, SEQ, D), 1.0),
        'c': nrm(ks[1], (BATCH, D), 1.0),
        'ada_w': nrm(ks[2], (DEPTH, 2, D, 3 * D), 0.5 * D ** -0.5),
        'ada_b': nrm(ks[3], (DEPTH, 2, 3 * D), 0.02),
        'norm_g': 1.0 + nrm(ks[4], (DEPTH, 2, D), 0.1),
        'pool_w': nrm(ks[5], (N_A, len(POOL_WINDOWS), POOL_GROUP, POOL_GROUP), POOL_GROUP ** -0.5),
        'pool_scale': 1.0 + nrm(ks[6], (N_A, D), 0.1),
        'kv_ada_w': nrm(ks[7], (D, 2 * D), 0.5 * D ** -0.5),
        'kv_ada_b': nrm(ks[8], (2 * D,), 0.02),
        'kv_norm_g': 1.0 + nrm(ks[9], (D,), 0.1),
        'w_k': nrm(ks[10], (D, H * E), D ** -0.5),
        'w_v': nrm(ks[11], (D, H * E), D ** -0.5),
        'k_norm_g': 1.0 + nrm(ks[12], (E,), 0.1),
        'w_q': nrm(ks[13], (N_B, D, N_DIL * H * E), D ** -0.5),
        'q_norm_g': 1.0 + nrm(ks[14], (N_B, E), 0.1),
        'w_o': nrm(ks[15], (N_B, H * E, D), (H * E) ** -0.5),
        'rel_bias': nrm(ks[16], (NUM_BUCKETS, N_DIL, H), 0.5),
        'ffn_w1': nrm(ks[17], (N_DENSE, D, D_FF_DENSE), D ** -0.5),
        'ffn_w3': nrm(ks[18], (N_DENSE, D, D_FF_DENSE), D ** -0.5),
        'ffn_w2': nrm(ks[19], (N_DENSE, D_FF_DENSE, D), D_FF_DENSE ** -0.5),
        'router_w': nrm(ks[20], (N_MOE, D, N_EXPERTS), D ** -0.5),
        'moe_w1': nrm(ks[21], (N_MOE, N_EXPERTS, D, D_FF_EXPERT), D ** -0.5),
        'moe_w3': nrm(ks[22], (N_MOE, N_EXPERTS, D, D_FF_EXPERT), D ** -0.5),
        'moe_w2': nrm(ks[23], (N_MOE, N_EXPERTS, D_FF_EXPERT, D), D_FF_EXPERT ** -0.5),
    }


def reference(x, c, ada_w, ada_b, norm_g, pool_w, pool_scale, kv_ada_w, kv_ada_b, kv_norm_g,
              w_k, w_v, k_norm_g, w_q, q_norm_g, w_o, rel_bias, ffn_w1, ffn_w3, ffn_w2,
              router_w, moe_w1, moe_w3, moe_w2):
    B, S, D = x.shape
    cond = jax.nn.silu(c)
    k = None
    v = None
    for i in range(DEPTH):
        if i == N_A:
            kv_shift, kv_scale = jnp.split(cond @ kv_ada_w + kv_ada_b, 2, axis=-1)
            kvh = modulate(rmsnorm(x, kv_norm_g), kv_shift, kv_scale)
            k = head_rms((kvh @ w_k).reshape(B, S, N_HEADS, HEAD_DIM), k_norm_g)
            v = (kvh @ w_v).reshape(B, S, N_HEADS, HEAD_DIM)
        shift, scale, gate = jnp.split(cond @ ada_w[i, 0] + ada_b[i, 0], 3, axis=-1)
        h = modulate(rmsnorm(x, norm_g[i, 0]), shift, scale)
        if i < N_A:
            y = pool_mixer(h, pool_w[i], pool_scale[i])
        else:
            j = i - N_A
            y = dilated_attention(h, k, v, w_q[j], q_norm_g[j], w_o[j], rel_bias)
        x = x + gate[:, None, :] * y
        shift, scale, gate = jnp.split(cond @ ada_w[i, 1] + ada_b[i, 1], 3, axis=-1)
        h = modulate(rmsnorm(x, norm_g[i, 1]), shift, scale)
        if i % 2 == 0:
            y = swiglu(h, ffn_w1[i // 2], ffn_w3[i // 2], ffn_w2[i // 2])
        else:
            y = moe_swiglu(h, router_w[i // 2], moe_w1[i // 2], moe_w3[i // 2], moe_w2[i // 2])
        x = x + gate[:, None, :] * y
    return x
```

```python
import functools
import math

import numpy as np
import jax
import jax.numpy as jnp
from jax import lax
from jax.experimental import pallas as pl
from jax.experimental.pallas import tpu as pltpu

F32 = jnp.float32
BF16 = jnp.bfloat16

V7X_LANES = 128
V7X_VMEM_LIMIT_BYTES = 58 * 1024 * 1024

HEAD_DIM = 128
BLOCK = 128
DILATIONS = (1, 4, 16)
N_TAPS = 128
NUM_BUCKETS = 32
MAX_DISTANCE = 2048
POOL_WINDOWS = (2, 4, 8, 16)
POOL_HALO = 16
TOP_K = 2
EPS = 1e-6
NEG_INF = -1e30
ATTN_TILE = BLOCK * DILATIONS[-1]


def _params(n_grid, vmem=V7X_VMEM_LIMIT_BYTES):
    return pltpu.CompilerParams(dimension_semantics=("arbitrary",) * n_grid, vmem_limit_bytes=vmem)


def _pick(n, pref, mult):
    if n <= pref:
        return n
    t = (pref // mult) * mult
    while t >= mult:
        if n % t == 0:
            return t
        t -= mult
    return n


def _silu(v):
    return v * jax.nn.sigmoid(v)


def _norm_mod(v, g, scale, shift):
    ms = jnp.mean(v * v, axis=-1, keepdims=True)
    return (v * lax.rsqrt(ms + EPS)) * g * (1.0 + scale) + shift


def _ada_body(c_ref, w_ref, b_ref, o_ref):
    cond = _silu(c_ref[...]).astype(BF16)
    o_ref[...] = jnp.dot(cond, w_ref[...].astype(BF16), preferred_element_type=F32) + b_ref[...]


def _ada(c8, w, b):
    m, d, n = w.shape
    bn = _pick(n, 512, V7X_LANES)
    return pl.pallas_call(
        _ada_body,
        out_shape=jax.ShapeDtypeStruct((m, 8, n), F32),
        grid=(m, n // bn),
        in_specs=[
            pl.BlockSpec((8, d), lambda i, j: (0, 0)),
            pl.BlockSpec((None, d, bn), lambda i, j: (i, 0, j)),
            pl.BlockSpec((None, 1, bn), lambda i, j: (i, 0, j)),
        ],
        out_specs=pl.BlockSpec((None, 8, bn), lambda i, j: (i, 0, j)),
        compiler_params=_params(2),
        name="ada",
    )(c8, w, b)


def _mixer_body(x_ref, halo_ref, mod_ref, par_ref, pw_ref, x1_ref, h2_ref, *, ts, group):
    i = pl.program_id(1)
    x = x_ref[...]
    g_a, g_f, pscale = par_ref[0:1, :], par_ref[1:2, :], par_ref[2:3, :]
    shift_a, scale_a, gate_a = mod_ref[0:1, :], mod_ref[1:2, :], mod_ref[2:3, :]
    shift_f, scale_f = mod_ref[3:4, :], mod_ref[4:5, :]

    h = _norm_mod(x, g_a, scale_a, shift_a)
    h_halo = jnp.where(i > 0, _norm_mod(halo_ref[...], g_a, scale_a, shift_a), 0.0)
    hh = jnp.concatenate([h_halo, h], axis=0)

    pos = i * ts + lax.broadcasted_iota(jnp.int32, (ts, 1), 0)
    ys = []
    for g, k in enumerate(POOL_WINDOWS):
        sl = slice(g * group, (g + 1) * group)
        a = hh[:, sl]
        span = 1
        while span < k:
            a = a[:-span] + a[span:]
            span *= 2
        start = POOL_HALO + 1 - k
        win = a[start:start + ts]
        cnt = jnp.minimum(pos + 1, k).astype(F32)
        p = win * (1.0 / cnt) - h[:, sl]
        ys.append(jnp.dot(p.astype(BF16), pw_ref[g], preferred_element_type=F32))
    y = jnp.concatenate(ys, axis=1) * pscale
    x1 = x + gate_a * y
    x1_ref[...] = x1
    h2_ref[...] = _norm_mod(x1, g_f, scale_f, shift_f).astype(BF16)


def _mixer(x, mod, par, pool_w):
    b, s, d = x.shape
    group = d // len(POOL_WINDOWS)
    ts = _pick(s, 256, POOL_HALO)
    hpb = ts // POOL_HALO
    return pl.pallas_call(
        functools.partial(_mixer_body, ts=ts, group=group),
        out_shape=(jax.ShapeDtypeStruct((b, s, d), F32), jax.ShapeDtypeStruct((b, s, d), BF16)),
        grid=(b, s // ts),
        in_specs=[
            pl.BlockSpec((None, ts, d), lambda bi, i: (bi, i, 0)),
            pl.BlockSpec((None, POOL_HALO, d), lambda bi, i: (bi, jnp.maximum(i * hpb - 1, 0), 0)),
            pl.BlockSpec((None, 8, d), lambda bi, i: (bi, 0, 0)),
            pl.BlockSpec((8, d), lambda bi, i: (0, 0)),
            pl.BlockSpec((len(POOL_WINDOWS), group, group), lambda bi, i: (0, 0, 0),
                         pipeline_mode=pl.Buffered(1)),
        ],
        out_specs=(pl.BlockSpec((None, ts, d), lambda bi, i: (bi, i, 0)),
                   pl.BlockSpec((None, ts, d), lambda bi, i: (bi, i, 0))),
        compiler_params=_params(2),
        name="mixer0",
    )(x, x, mod, par, pool_w)


def _norm2_body(x_ref, mod_ref, par_ref, a_ref, b_ref):
    x = x_ref[...]
    a_ref[...] = _norm_mod(x, par_ref[0:1, :], mod_ref[1:2, :], mod_ref[0:1, :]).astype(BF16)
    b_ref[...] = _norm_mod(x, par_ref[1:2, :], mod_ref[3:4, :], mod_ref[2:3, :]).astype(BF16)


def _norm2(x, mod, par):
    b, s, d = x.shape
    ts = _pick(s, 512, 16)
    row = pl.BlockSpec((None, ts, d), lambda bi, i: (bi, i, 0))
    return pl.pallas_call(
        _norm2_body,
        out_shape=(jax.ShapeDtypeStruct((b, s, d), BF16), jax.ShapeDtypeStruct((b, s, d), BF16)),
        grid=(b, s // ts),
        in_specs=[row, pl.BlockSpec((None, 8, d), lambda bi, i: (bi, 0, 0)),
                  pl.BlockSpec((8, d), lambda bi, i: (0, 0))],
        out_specs=(row, row),
        compiler_params=_params(2),
        name="norm_kv_q",
    )(x, mod, par)


def _route_body(x_ref, mod_ref, par_ref, rw_ref, hp_ref, route_ref, *, n_exp):
    x = x_ref[...]
    h = _norm_mod(x, par_ref[0:1, :], mod_ref[1:2, :], mod_ref[0:1, :])
    d = h.shape[1]
    bits = pltpu.bitcast(h.astype(BF16).astype(F32), jnp.uint32)
    hp_ref[...] = (bits[:, : d // 2] >> 16) | (bits[:, d // 2:] & jnp.uint32(0xFFFF0000))
    logits = jnp.dot(h, rw_ref[...], preferred_element_type=F32, precision=lax.Precision.HIGHEST)
    lane = lax.broadcasted_iota(jnp.int32, logits.shape, 1)
    lane_f = lane.astype(F32)
    lg = jnp.where(lane < n_exp, logits, -jnp.inf)
    v1 = jnp.max(lg, axis=1, keepdims=True)
    i1 = jnp.min(jnp.where(lg == v1, lane_f, float(V7X_LANES)), axis=1, keepdims=True)
    lg2 = jnp.where(lane_f == i1, -jnp.inf, lg)
    v2 = jnp.max(lg2, axis=1, keepdims=True)
    i2 = jnp.min(jnp.where(lg2 == v2, lane_f, float(V7X_LANES)), axis=1, keepdims=True)
    e2 = jnp.exp(v2 - v1)
    g1 = 1.0 / (1.0 + e2)
    g2 = e2 / (1.0 + e2)
    out = jnp.where(lane == 0, i1, 0.0)
    out = jnp.where(lane == 1, i2, out)
    out = jnp.where(lane == 2, g1, out)
    out = jnp.where(lane == 3, g2, out)
    route_ref[...] = out


def _route(x, mod, par, rw_pad, n_exp):
    b, s, d = x.shape
    ts = _pick(s, 256, 16)
    return pl.pallas_call(
        functools.partial(_route_body, n_exp=n_exp),
        out_shape=(jax.ShapeDtypeStruct((b, s, d // 2), jnp.uint32),
                   jax.ShapeDtypeStruct((b, s, V7X_LANES), F32)),
        grid=(b, s // ts),
        in_specs=[pl.BlockSpec((None, ts, d), lambda bi, i: (bi, i, 0)),
                  pl.BlockSpec((None, 8, d), lambda bi, i: (bi, 0, 0)),
                  pl.BlockSpec((8, d), lambda bi, i: (0, 0)),
                  pl.BlockSpec((d, V7X_LANES), lambda bi, i: (0, 0))],
        out_specs=(pl.BlockSpec((None, ts, d // 2), lambda bi, i: (bi, i, 0)),
                   pl.BlockSpec((None, ts, V7X_LANES), lambda bi, i: (bi, i, 0))),
        compiler_params=_params(2),
        name="norm_route",
    )(x, mod, par, rw_pad)


def _up_body(a_ref, w1_ref, w3_ref, o_ref):
    a = a_ref[...]
    h1 = jnp.dot(a, w1_ref[...].astype(BF16), preferred_element_type=F32)
    h3 = jnp.dot(a, w3_ref[...].astype(BF16), preferred_element_type=F32)
    o_ref[...] = (_silu(h1) * h3).astype(BF16)


def _up_grouped_body(te_ref, a_ref, w1_ref, w3_ref, o_ref):
    del te_ref
    _up_body(a_ref, w1_ref, w3_ref, o_ref)


def _swiglu_up(a, w1, w3):
    m, k = a.shape
    f = w1.shape[1]
    tm = _pick(m, 1024, 16)
    bn = _pick(f, 256, V7X_LANES)
    return pl.pallas_call(
        _up_body,
        out_shape=jax.ShapeDtypeStruct((m, f), BF16),
        grid=(m // tm, f // bn),
        in_specs=[pl.BlockSpec((tm, k), lambda i, j: (i, 0)),
                  pl.BlockSpec((k, bn), lambda i, j: (0, j)),
                  pl.BlockSpec((k, bn), lambda i, j: (0, j))],
        out_specs=pl.BlockSpec((tm, bn), lambda i, j: (i, j)),
        compiler_params=_params(2),
        name="ffn_up",
    )(a, w1, w3)


def _swiglu_up_grouped(tile_expert, a, w1, w3, tm):
    m, k = a.shape
    f = w1.shape[2]
    bn = _pick(f, 512, V7X_LANES)
    gs = pltpu.PrefetchScalarGridSpec(
        num_scalar_prefetch=1,
        grid=(f // bn, m // tm),
        in_specs=[pl.BlockSpec((tm, k), lambda j, i, te: (i, 0)),
                  pl.BlockSpec((None, k, bn), lambda j, i, te: (te[i], 0, j)),
                  pl.BlockSpec((None, k, bn), lambda j, i, te: (te[i], 0, j))],
        out_specs=pl.BlockSpec((tm, bn), lambda j, i, te: (i, j)),
    )
    return pl.pallas_call(
        _up_grouped_body,
        out_shape=jax.ShapeDtypeStruct((m, f), BF16),
        grid_spec=gs,
        compiler_params=_params(2),
        name="moe_up",
    )(tile_expert, a, w1, w3)


def _down_res_body(a_ref, w_ref, x_ref, mod_ref, o_ref, *, gate_row):
    y = jnp.dot(a_ref[...], w_ref[...].astype(BF16), preferred_element_type=F32)
    o_ref[...] = x_ref[...] + mod_ref[gate_row:gate_row + 1, :] * y


def _matmul_residual(a, w, xres, mod, gate_row, tm_pref, bn_pref, name):
    m, k = a.shape
    n = w.shape[1]
    b = mod.shape[0]
    s = m // b
    tm = _pick(s, tm_pref, 16)
    bn = _pick(n, bn_pref, V7X_LANES)
    return pl.pallas_call(
        functools.partial(_down_res_body, gate_row=gate_row),
        out_shape=jax.ShapeDtypeStruct((m, n), F32),
        grid=(m // tm, n // bn),
        in_specs=[pl.BlockSpec((tm, k), lambda i, j: (i, 0)),
                  pl.BlockSpec((k, bn), lambda i, j: (0, j)),
                  pl.BlockSpec((tm, bn), lambda i, j: (i, j)),
                  pl.BlockSpec((None, 8, bn), lambda i, j: ((i * tm) // s, 0, j))],
        out_specs=pl.BlockSpec((tm, bn), lambda i, j: (i, j)),
        compiler_params=_params(2),
        name=name,
    )(a, w, xres, mod)


def _down_grouped_body(te_ref, a_ref, w_ref, o_ref):
    del te_ref
    o_ref[...] = jnp.dot(a_ref[...], w_ref[...].astype(BF16), preferred_element_type=F32)


def _down_grouped(tile_expert, a, w, tm):
    m, k = a.shape
    n = w.shape[2]
    bn = _pick(n, 1024, V7X_LANES)
    gs = pltpu.PrefetchScalarGridSpec(
        num_scalar_prefetch=1,
        grid=(n // bn, m // tm),
        in_specs=[pl.BlockSpec((tm, k), lambda j, i, te: (i, 0)),
                  pl.BlockSpec((None, k, bn), lambda j, i, te: (te[i], 0, j))],
        out_specs=pl.BlockSpec((tm, bn), lambda j, i, te: (i, j)),
    )
    return pl.pallas_call(
        _down_grouped_body,
        out_shape=jax.ShapeDtypeStruct((m, n), F32),
        grid_spec=gs,
        compiler_params=_params(2),
        name="moe_down",
    )(tile_expert, a, w)


def _proj_body(a_ref, w_ref, g_ref, *rest, dils, normed, out_scale, tm):
    o_refs, scr = rest[:len(dils)], rest[len(dils)]
    y = jnp.dot(a_ref[...], w_ref[...].astype(BF16), preferred_element_type=F32)
    n_slab = y.shape[1] // HEAD_DIM
    gvec = g_ref[...] * out_scale
    for s in range(n_slab):
        ys = y[:, s * HEAD_DIM:(s + 1) * HEAD_DIM]
        if normed:
            ms = jnp.mean(ys * ys, axis=-1, keepdims=True)
            ys = (ys * lax.rsqrt(ms + EPS)) * gvec
        scr[s] = ys
    for o_ref, d in zip(o_refs, dils):
        for s in range(n_slab):
            if d == 1:
                o_ref[s, 0] = scr[s].astype(BF16)
            else:
                for r in range(d):
                    o_ref[s, r] = scr[s, pl.ds(r, tm // d, stride=d), :].astype(BF16)


def _proj_heads(a, w, col0, gvec, dils, normed, out_scale, batch, name):
    m, k = a.shape
    s = m // batch
    d_model = k
    n_heads = d_model // HEAD_DIM
    tm = _pick(s, 1024, 16 * max(dils))
    bn = _pick(d_model, 512, HEAD_DIM)
    hpb = bn // HEAD_DIM
    tpb = s // tm
    cb0 = col0 // bn
    outs = tuple(jax.ShapeDtypeStruct((batch, n_heads, d, s // d, HEAD_DIM), BF16) for d in dils)
    out_specs = tuple(
        pl.BlockSpec((None, hpb, d, tm // d, HEAD_DIM), lambda i, j: (i // tpb, j, 0, i % tpb, 0)) for d in dils)
    return pl.pallas_call(
        functools.partial(_proj_body, dils=dils, normed=normed, out_scale=out_scale, tm=tm),
        out_shape=outs,
        grid=(m // tm, d_model // bn),
        in_specs=[pl.BlockSpec((tm, k), lambda i, j: (i, 0)),
                  pl.BlockSpec((k, bn), lambda i, j: (0, cb0 + j)),
                  pl.BlockSpec((1, HEAD_DIM), lambda i, j: (0, 0))],
        out_specs=out_specs,
        scratch_shapes=[pltpu.VMEM((hpb, tm, HEAD_DIM), F32)],
        compiler_params=_params(2),
        name=name,
    )(a, w, gvec)


def _band_buckets():
    i = np.arange(BLOCK)[:, None]
    kk = np.arange(2 * BLOCK)[None, :]
    rel = i + BLOCK - kk
    valid = (rel >= 0) & (rel <= N_TAPS)
    max_exact = NUM_BUCKETS // 2
    out = []
    for d in DILATIONS:
        n = np.maximum(rel, 0) * d
        nf = np.maximum(n, 1).astype(np.float32)
        large = max_exact + (np.log(nf / np.float32(max_exact)) / np.float32(math.log(MAX_DISTANCE / max_exact))
                             * np.float32(NUM_BUCKETS - max_exact)).astype(np.int32)
        bucket = np.where(n < max_exact, n, np.minimum(large, NUM_BUCKETS - 1))
        out.append(np.where(valid, bucket, -1))
    return np.stack(out).astype(np.int32)


def _bias_body(rb_ref, bk_ref, o_ref, *, n_heads):
    g = pl.program_id(0)
    h = pl.program_id(1)
    bucket = bk_ref[...]
    acc = jnp.full(bucket.shape, NEG_INF, F32)
    for b in range(NUM_BUCKETS):
        acc = jnp.where(bucket == b, rb_ref[(b * len(DILATIONS) + g) * n_heads + h], acc)
    o_ref[...] = acc


def _bias_bands(rel_bias, n_heads):
    buckets = jnp.asarray(_band_buckets())
    n_g = len(DILATIONS)
    return pl.pallas_call(
        functools.partial(_bias_body, n_heads=n_heads),
        out_shape=jax.ShapeDtypeStruct((n_g, n_heads, BLOCK, 2 * BLOCK), F32),
        grid=(n_g, n_heads),
        in_specs=[pl.BlockSpec(memory_space=pltpu.SMEM),
                  pl.BlockSpec((None, BLOCK, 2 * BLOCK), lambda g, h: (g, 0, 0))],
        out_specs=pl.BlockSpec((None, None, BLOCK, 2 * BLOCK), lambda g, h: (g, h, 0, 0)),
        compiler_params=_params(2),
        name="bias_bands",
    )(rel_bias.reshape(-1), buckets)


def _attn_body(*refs):
    n_g = len(DILATIONS)
    q_refs = refs[0:n_g]
    k_refs = refs[n_g:3 * n_g:2]
    kh_refs = refs[n_g + 1:3 * n_g:2]
    v_refs = refs[3 * n_g:5 * n_g:2]
    vh_refs = refs[3 * n_g + 1:5 * n_g:2]
    bias_ref = refs[5 * n_g]
    o_ref = refs[5 * n_g + 1]
    scr = refs[5 * n_g + 2:]
    kcats, vcats = scr[0:n_g], scr[n_g:2 * n_g]
    bsc, oscr, lscr = scr[2 * n_g], scr[2 * n_g + 1], scr[2 * n_g + 2]

    ti = pl.program_id(2)
    col = lax.broadcasted_iota(jnp.int32, (BLOCK, 2 * BLOCK), 1)
    pen = jnp.where((col < BLOCK) & (ti == 0), NEG_INF, 0.0)
    for g in range(n_g):
        band = bias_ref[g]
        bsc[0, g] = band
        bsc[1, g] = band + pen

    for g, d in enumerate(DILATIONS):
        n = ATTN_TILE // d
        nj = n // BLOCK
        kcat, vcat = kcats[g], vcats[g]
        kcat[:, 0:BLOCK, :] = kh_refs[g][...]
        kcat[:, BLOCK:, :] = k_refs[g][...]
        vcat[:, 0:BLOCK, :] = vh_refs[g][...]
        vcat[:, BLOCK:, :] = v_refs[g][...]
        q_ref = q_refs[g]

        def unit(u, carry, g=g, d=d, nj=nj, kcat=kcat, vcat=vcat, q_ref=q_ref):
            r = u // nj
            j = u % nj
            row = pl.multiple_of(j * BLOCK, BLOCK)
            q = q_ref[r, pl.ds(row, BLOCK), :]
            kk = kcat[r, pl.ds(row, 2 * BLOCK), :]
            vv = vcat[r, pl.ds(row, 2 * BLOCK), :]
            s = lax.dot_general(q, kk, (((1,), (1,)), ((), ())), preferred_element_type=F32)
            s = s + bsc[jnp.where(j == 0, 1, 0), g]
            m = jnp.max(s, axis=1, keepdims=True)
            p = jnp.exp(s - m)
            l = jnp.sum(p, axis=1, keepdims=True)
            acc = jnp.dot(p.astype(BF16), vv, preferred_element_type=F32)
            o = acc * (1.0 / l)
            lse = jnp.broadcast_to(m + jnp.log(l), (BLOCK, HEAD_DIM))
            start = j * (BLOCK * d) + r
            if d == 1:
                rows = pl.ds(pl.multiple_of(start, BLOCK), BLOCK)
            else:
                rows = pl.ds(start, BLOCK, stride=d)
            oscr[g, rows, :] = o
            lscr[g, rows, :] = lse
            return carry

        lax.fori_loop(0, d * nj, unit, 0)

    chunk = 2 * BLOCK

    def merge(c, carry):
        rows = pl.ds(pl.multiple_of(c * chunk, chunk), chunk)
        ls = [lscr[g, rows, :] for g in range(n_g)]
        mx = functools.reduce(jnp.maximum, ls)
        es = [jnp.exp(l - mx) for l in ls]
        num = sum(e * oscr[g, rows, :] for g, e in enumerate(es))
        den = sum(es)
        o_ref[rows, :] = (num * (1.0 / den)).astype(BF16)
        return carry

    lax.fori_loop(0, ATTN_TILE // chunk, merge, 0)


def _attention(qs, ks, vs, bias, batch, seq, n_heads):
    n_g = len(DILATIONS)
    in_specs, args = [], []
    for g, d in enumerate(DILATIONS):
        n = ATTN_TILE // d
        in_specs.append(pl.BlockSpec((None, None, d, n, HEAD_DIM), lambda b, h, t: (b, h, 0, t, 0)))
        args.append(qs[g])
    for src in (ks, vs):
        for g, d in enumerate(DILATIONS):
            n = ATTN_TILE // d
            nb = n // BLOCK
            in_specs.append(pl.BlockSpec((None, None, d, n, HEAD_DIM), lambda b, h, t: (b, h, 0, t, 0)))
            in_specs.append(pl.BlockSpec((None, None, d, BLOCK, HEAD_DIM),
                                         lambda b, h, t, nb=nb: (b, h, 0, jnp.maximum(t * nb - 1, 0), 0)))
            args += [src[g], src[g]]
    in_specs.append(pl.BlockSpec((n_g, None, BLOCK, 2 * BLOCK), lambda b, h, t: (0, h, 0, 0)))
    args.append(bias)
    scratch = [pltpu.VMEM((d, BLOCK + ATTN_TILE // d, HEAD_DIM), BF16) for d in DILATIONS] * 2
    scratch += [pltpu.VMEM((2, n_g, BLOCK, 2 * BLOCK), F32),
                pltpu.VMEM((n_g, ATTN_TILE, HEAD_DIM), F32),
                pltpu.VMEM((n_g, ATTN_TILE, HEAD_DIM), F32)]
    return pl.pallas_call(
        _attn_body,
        out_shape=jax.ShapeDtypeStruct((batch, seq, n_heads * HEAD_DIM), BF16),
        grid=(batch, n_heads, seq // ATTN_TILE),
        in_specs=in_specs,
        out_specs=pl.BlockSpec((None, ATTN_TILE, HEAD_DIM), lambda b, h, t: (b, t, h)),
        scratch_shapes=scratch,
        compiler_params=_params(3),
        name="dilated_attn",
    )(*args)


def _gather_body(idx_ref, src_ref, o_ref, buf, sem, *, rb):
    def issue(r, c):
        pltpu.make_async_copy(src_ref.at[pl.ds(idx_ref[0, r], 1)], buf.at[pl.ds(r, 1)], sem).start()
        return c

    lax.fori_loop(0, rb, issue, 0)

    def drain(r, c):
        pltpu.make_async_copy(src_ref.at[pl.ds(0, 1)], buf.at[pl.ds(r, 1)], sem).wait()
        return c

    lax.fori_loop(0, rb, drain, 0)
    u = buf[...]
    half = u.shape[1]
    o_ref[:, :half] = pltpu.bitcast(u << 16, F32).astype(BF16)
    o_ref[:, half:] = pltpu.bitcast(u & jnp.uint32(0xFFFF0000), F32).astype(BF16)


def _gather_rows(src_idx, hp, rb):
    n_rows = src_idx.shape[0]
    half = hp.shape[1]
    idx3 = src_idx.reshape(n_rows // rb, 1, rb)
    return pl.pallas_call(
        functools.partial(_gather_body, rb=rb),
        out_shape=jax.ShapeDtypeStruct((n_rows, 2 * half), BF16),
        grid=(n_rows // rb,),
        in_specs=[pl.BlockSpec((None, 1, rb), lambda i: (i, 0, 0), memory_space=pltpu.SMEM),
                  pl.BlockSpec(memory_space=pl.ANY)],
        out_specs=pl.BlockSpec((rb, 2 * half), lambda i: (i, 0)),
        scratch_shapes=[pltpu.VMEM((rb, half), jnp.uint32), pltpu.SemaphoreType.DMA(())],
        compiler_params=_params(1),
        name="moe_gather",
    )(idx3, hp)


def _combine_body(pos_ref, y_ref, x_ref, route_ref, mod_ref, o_ref, buf, sem, *, tc, gate_row):
    def issue(r, c):
        pltpu.make_async_copy(y_ref.at[pl.ds(pos_ref[0, r], 1)], buf.at[0, pl.ds(r, 1)], sem).start()
        pltpu.make_async_copy(y_ref.at[pl.ds(pos_ref[0, tc + r], 1)], buf.at[1, pl.ds(r, 1)], sem).start()
        return c

    lax.fori_loop(0, tc, issue, 0)

    def drain(r, c):
        pltpu.make_async_copy(y_ref.at[pl.ds(0, 1)], buf.at[0, pl.ds(r, 1)], sem).wait()
        pltpu.make_async_copy(y_ref.at[pl.ds(0, 1)], buf.at[1, pl.ds(r, 1)], sem).wait()
        return c

    lax.fori_loop(0, tc, drain, 0)
    g1 = route_ref[:, 2:3]
    g2 = route_ref[:, 3:4]
    y = g1 * buf[0] + g2 * buf[1]
    o_ref[...] = x_ref[...] + mod_ref[gate_row:gate_row + 1, :] * y


def _combine(pos, y, x, route, mod, gate_row):
    m, d = x.shape
    b = mod.shape[0]
    s = m // b
    tc = pos.shape[2] // 2
    return pl.pallas_call(
        functools.partial(_combine_body, tc=tc, gate_row=gate_row),
        out_shape=jax.ShapeDtypeStruct((m, d), F32),
        grid=(m // tc,),
        in_specs=[pl.BlockSpec((None, 1, 2 * tc), lambda i: (i, 0, 0), memory_space=pltpu.SMEM),
                  pl.BlockSpec(memory_space=pl.ANY),
                  pl.BlockSpec((tc, d), lambda i: (i, 0)),
                  pl.BlockSpec((tc, V7X_LANES), lambda i: (i, 0)),
                  pl.BlockSpec((None, 8, d), lambda i: ((i * tc) // s, 0, 0))],
        out_specs=pl.BlockSpec((tc, d), lambda i: (i, 0)),
        scratch_shapes=[pltpu.VMEM((2, tc, d), F32), pltpu.SemaphoreType.DMA(())],
        compiler_params=_params(1),
        name="moe_combine",
    )(pos, y, x, route, mod)


def _rows8(*rows):
    b, d = rows[0].shape
    pad = [jnp.zeros((b, d), F32)] * (8 - len(rows))
    return jnp.stack(list(rows) + pad, axis=1)


def _par8(*rows):
    d = rows[0].shape[0]
    pad = [jnp.zeros((d,), F32)] * (8 - len(rows))
    return jnp.stack(list(rows) + pad, axis=0)


def kernel(x, c, ada_w, ada_b, norm_g, pool_w, pool_scale, kv_ada_w, kv_ada_b, kv_norm_g, w_k, w_v, k_norm_g,
           w_q, q_norm_g, w_o, rel_bias, ffn_w1, ffn_w3, ffn_w2, router_w, moe_w1, moe_w3, moe_w2):
    batch, seq, d = x.shape
    n_tok = batch * seq
    n_heads = d // HEAD_DIM
    n_exp = router_w.shape[2]
    assert d % HEAD_DIM == 0 and seq % ATTN_TILE == 0 and batch <= 8

    c8 = jnp.pad(c, ((0, 8 - batch), (0, 0)))
    mods = _ada(c8, ada_w.reshape(4, d, 3 * d), ada_b.reshape(4, 1, 3 * d))[:, :batch]
    kvm = _ada(c8, kv_ada_w[None], kv_ada_b[None, None])[0, :batch]
    sh = lambda m: (mods[m, :, :d], mods[m, :, d:2 * d], mods[m, :, 2 * d:])
    (s00, c00, g00), (s01, c01, g01), (s10, c10, g10), (s11, c11, g11) = sh(0), sh(1), sh(2), sh(3)
    kv_shift, kv_scale = kvm[:, :d], kvm[:, d:]

    x1, h2 = _mixer(x, _rows8(s00, c00, g00, s01, c01), _par8(norm_g[0, 0], norm_g[0, 1], pool_scale[0]),
                    pool_w[0].astype(BF16))
    gact = _swiglu_up(h2.reshape(n_tok, d), ffn_w1[0].astype(BF16), ffn_w3[0].astype(BF16))
    x2 = _matmul_residual(gact, ffn_w2[0].astype(BF16), x1.reshape(n_tok, d), _rows8(g01), 0, 512, 512, "ffn_down")

    kvh, h3 = _norm2(x2.reshape(batch, seq, d), _rows8(kv_shift, kv_scale, s10, c10), _par8(kv_norm_g, norm_g[1, 0]))
    kvh, h3 = kvh.reshape(n_tok, d), h3.reshape(n_tok, d)
    ks = _proj_heads(kvh, w_k.astype(BF16), 0, k_norm_g.reshape(1, HEAD_DIM), DILATIONS, True, 1.0, batch, "proj_k")
    vs = _proj_heads(kvh, w_v.astype(BF16), 0, k_norm_g.reshape(1, HEAD_DIM), DILATIONS, False, 1.0, batch, "proj_v")
    wq = w_q[0].astype(BF16)
    qs = [_proj_heads(h3, wq, g * d, q_norm_g[0].reshape(1, HEAD_DIM), (dl,), True, HEAD_DIM ** -0.5, batch,
                      f"proj_q{dl}")[0] for g, dl in enumerate(DILATIONS)]

    bias = _bias_bands(rel_bias, n_heads)
    att = _attention(qs, ks, vs, bias, batch, seq, n_heads)
    x3 = _matmul_residual(att.reshape(n_tok, d), w_o[0].astype(BF16), x2, _rows8(g10), 0, 1024, 512, "attn_out")

    rw_pad = jnp.pad(router_w[0], ((0, 0), (0, V7X_LANES - n_exp)))
    hp, route = _route(x3.reshape(batch, seq, d), _rows8(s11, c11), _par8(norm_g[1, 1]), rw_pad, n_exp)
    hp, route = hp.reshape(n_tok, d // 2), route.reshape(n_tok, V7X_LANES)

    tm = _pick(n_tok, 512, 256)
    r_pad = TOP_K * n_tok + n_exp * tm
    e_flat = jnp.concatenate([route[:, 0], route[:, 1]]).astype(jnp.int32)
    onehot = (e_flat[:, None] == jnp.arange(n_exp, dtype=jnp.int32)[None, :]).astype(jnp.int32)
    csum = jnp.cumsum(onehot, axis=0)
    rank = jnp.sum((csum - 1) * onehot, axis=1)
    padded = ((csum[-1] + tm - 1) // tm) * tm
    gend = jnp.cumsum(padded)
    dest = (gend - padded)[e_flat] + rank
    tok = jnp.arange(TOP_K * n_tok, dtype=jnp.int32) % n_tok
    src_idx = jnp.zeros((r_pad,), jnp.int32).at[dest].set(tok)
    tile_expert = jnp.minimum(
        jnp.searchsorted(gend, jnp.arange(r_pad // tm, dtype=jnp.int32) * tm, side="right"), n_exp - 1
    ).astype(jnp.int32)

    xg = _gather_rows(src_idx, hp, 256)
    hg = _swiglu_up_grouped(tile_expert, xg, moe_w1[0].astype(BF16), moe_w3[0].astype(BF16), tm)
    yg = _down_grouped(tile_expert, hg, moe_w2[0].astype(BF16), tm)

    tc = _pick(n_tok, 256, 8)
    pos = jnp.concatenate([dest[:n_tok].reshape(n_tok // tc, 1, tc), dest[n_tok:].reshape(n_tok // tc, 1, tc)], axis=2)
    out = _combine(pos.astype(jnp.int32), yg, x3, route, _rows8(g11), 0)
    return out.reshape(batch, seq, d)
```

```python
import functools
import math

import numpy as np
import jax
import jax.numpy as jnp
from jax import lax
from jax.experimental import pallas as pl
from jax.experimental.pallas import tpu as pltpu

F32 = jnp.float32
BF16 = jnp.bfloat16

V7X_LANES = 128
V7X_VMEM_LIMIT_BYTES = 58 * 1024 * 1024

HEAD_DIM = 128
BLOCK = 128
DILATIONS = (1, 4, 16)
N_TAPS = 128
NUM_BUCKETS = 32
MAX_DISTANCE = 2048
POOL_WINDOWS = (2, 4, 8, 16)
POOL_HALO = 16
TOP_K = 2
EPS = 1e-6
NEG_INF = -1e30
ATTN_TILE = BLOCK * DILATIONS[-1]
ATTN_UNROLL = 16


def _params(n_grid, vmem=V7X_VMEM_LIMIT_BYTES):
    return pltpu.CompilerParams(dimension_semantics=("arbitrary",) * n_grid, vmem_limit_bytes=vmem)


def _pick(n, pref, mult):
    if n <= pref:
        return n
    t = (pref // mult) * mult
    while t >= mult:
        if n % t == 0:
            return t
        t -= mult
    return n


def _silu(v):
    return v * jax.nn.sigmoid(v)


def _norm_mod(v, g, scale, shift):
    ms = jnp.mean(v * v, axis=-1, keepdims=True)
    return (v * lax.rsqrt(ms + EPS)) * g * (1.0 + scale) + shift


def _ada_body(c_ref, w_ref, b_ref, o_ref):
    cond = _silu(c_ref[...]).astype(BF16)
    o_ref[...] = jnp.dot(cond, w_ref[...].astype(BF16), preferred_element_type=F32) + b_ref[...]


def _ada(c8, w, b):
    m, d, n = w.shape
    bn = _pick(n, 512, V7X_LANES)
    return pl.pallas_call(
        _ada_body,
        out_shape=jax.ShapeDtypeStruct((m, 8, n), F32),
        grid=(m, n // bn),
        in_specs=[
            pl.BlockSpec((8, d), lambda i, j: (0, 0)),
            pl.BlockSpec((None, d, bn), lambda i, j: (i, 0, j)),
            pl.BlockSpec((None, 1, bn), lambda i, j: (i, 0, j)),
        ],
        out_specs=pl.BlockSpec((None, 8, bn), lambda i, j: (i, 0, j)),
        compiler_params=_params(2),
        name="ada",
    )(c8, w, b)


def _mixer_body(x_ref, halo_ref, mod_ref, par_ref, pw_ref, x1_ref, h2_ref, *, ts, group):
    i = pl.program_id(1)
    x = x_ref[...]
    g_a, g_f, pscale = par_ref[0:1, :], par_ref[1:2, :], par_ref[2:3, :]
    shift_a, scale_a, gate_a = mod_ref[0:1, :], mod_ref[1:2, :], mod_ref[2:3, :]
    shift_f, scale_f = mod_ref[3:4, :], mod_ref[4:5, :]

    h = _norm_mod(x, g_a, scale_a, shift_a)
    h_halo = jnp.where(i > 0, _norm_mod(halo_ref[...], g_a, scale_a, shift_a), 0.0)
    hh = jnp.concatenate([h_halo, h], axis=0)

    pos = i * ts + lax.broadcasted_iota(jnp.int32, (ts, 1), 0)
    ys = []
    for g, k in enumerate(POOL_WINDOWS):
        sl = slice(g * group, (g + 1) * group)
        a = hh[:, sl]
        span = 1
        while span < k:
            a = a[:-span] + a[span:]
            span *= 2
        start = POOL_HALO + 1 - k
        win = a[start:start + ts]
        cnt = jnp.minimum(pos + 1, k).astype(F32)
        p = win * (1.0 / cnt) - h[:, sl]
        ys.append(jnp.dot(p.astype(BF16), pw_ref[g], preferred_element_type=F32))
    y = jnp.concatenate(ys, axis=1) * pscale
    x1 = x + gate_a * y
    x1_ref[...] = x1
    h2_ref[...] = _norm_mod(x1, g_f, scale_f, shift_f).astype(BF16)


def _mixer(x, mod, par, pool_w):
    b, s, d = x.shape
    group = d // len(POOL_WINDOWS)
    ts = _pick(s, 256, POOL_HALO)
    hpb = ts // POOL_HALO
    return pl.pallas_call(
        functools.partial(_mixer_body, ts=ts, group=group),
        out_shape=(jax.ShapeDtypeStruct((b, s, d), F32), jax.ShapeDtypeStruct((b, s, d), BF16)),
        grid=(b, s // ts),
        in_specs=[
            pl.BlockSpec((None, ts, d), lambda bi, i: (bi, i, 0)),
            pl.BlockSpec((None, POOL_HALO, d), lambda bi, i: (bi, jnp.maximum(i * hpb - 1, 0), 0)),
            pl.BlockSpec((None, 8, d), lambda bi, i: (bi, 0, 0)),
            pl.BlockSpec((8, d), lambda bi, i: (0, 0)),
            pl.BlockSpec((len(POOL_WINDOWS), group, group), lambda bi, i: (0, 0, 0),
                         pipeline_mode=pl.Buffered(1)),
        ],
        out_specs=(pl.BlockSpec((None, ts, d), lambda bi, i: (bi, i, 0)),
                   pl.BlockSpec((None, ts, d), lambda bi, i: (bi, i, 0))),
        compiler_params=_params(2),
        name="mixer0",
    )(x, x, mod, par, pool_w)


def _norm2_body(x_ref, mod_ref, par_ref, a_ref, b_ref):
    x = x_ref[...]
    a_ref[...] = _norm_mod(x, par_ref[0:1, :], mod_ref[1:2, :], mod_ref[0:1, :]).astype(BF16)
    b_ref[...] = _norm_mod(x, par_ref[1:2, :], mod_ref[3:4, :], mod_ref[2:3, :]).astype(BF16)


def _norm2(x, mod, par):
    b, s, d = x.shape
    ts = _pick(s, 512, 16)
    row = pl.BlockSpec((None, ts, d), lambda bi, i: (bi, i, 0))
    return pl.pallas_call(
        _norm2_body,
        out_shape=(jax.ShapeDtypeStruct((b, s, d), BF16), jax.ShapeDtypeStruct((b, s, d), BF16)),
        grid=(b, s // ts),
        in_specs=[row, pl.BlockSpec((None, 8, d), lambda bi, i: (bi, 0, 0)),
                  pl.BlockSpec((8, d), lambda bi, i: (0, 0))],
        out_specs=(row, row),
        compiler_params=_params(2),
        name="norm_kv_q",
    )(x, mod, par)


def _route_body(x_ref, mod_ref, par_ref, rw_ref, hp_ref, route_ref, *, n_exp):
    x = x_ref[...]
    h = _norm_mod(x, par_ref[0:1, :], mod_ref[1:2, :], mod_ref[0:1, :])
    d = h.shape[1]
    bits = pltpu.bitcast(h.astype(BF16).astype(F32), jnp.uint32)
    hp_ref[...] = (bits[:, : d // 2] >> 16) | (bits[:, d // 2:] & jnp.uint32(0xFFFF0000))
    logits = jnp.dot(h, rw_ref[...], preferred_element_type=F32, precision=lax.Precision.HIGHEST)
    lane = lax.broadcasted_iota(jnp.int32, logits.shape, 1)
    lane_f = lane.astype(F32)
    lg = jnp.where(lane < n_exp, logits, -jnp.inf)
    v1 = jnp.max(lg, axis=1, keepdims=True)
    i1 = jnp.min(jnp.where(lg == v1, lane_f, float(V7X_LANES)), axis=1, keepdims=True)
    lg2 = jnp.where(lane_f == i1, -jnp.inf, lg)
    v2 = jnp.max(lg2, axis=1, keepdims=True)
    i2 = jnp.min(jnp.where(lg2 == v2, lane_f, float(V7X_LANES)), axis=1, keepdims=True)
    e2 = jnp.exp(v2 - v1)
    g1 = 1.0 / (1.0 + e2)
    g2 = e2 / (1.0 + e2)
    out = jnp.where(lane == 0, i1, 0.0)
    out = jnp.where(lane == 1, i2, out)
    out = jnp.where(lane == 2, g1, out)
    out = jnp.where(lane == 3, g2, out)
    route_ref[...] = out


def _route(x, mod, par, rw_pad, n_exp):
    b, s, d = x.shape
    ts = _pick(s, 256, 16)
    return pl.pallas_call(
        functools.partial(_route_body, n_exp=n_exp),
        out_shape=(jax.ShapeDtypeStruct((b, s, d // 2), jnp.uint32),
                   jax.ShapeDtypeStruct((b, s, V7X_LANES), F32)),
        grid=(b, s // ts),
        in_specs=[pl.BlockSpec((None, ts, d), lambda bi, i: (bi, i, 0)),
                  pl.BlockSpec((None, 8, d), lambda bi, i: (bi, 0, 0)),
                  pl.BlockSpec((8, d), lambda bi, i: (0, 0)),
                  pl.BlockSpec((d, V7X_LANES), lambda bi, i: (0, 0))],
        out_specs=(pl.BlockSpec((None, ts, d // 2), lambda bi, i: (bi, i, 0)),
                   pl.BlockSpec((None, ts, V7X_LANES), lambda bi, i: (bi, i, 0))),
        compiler_params=_params(2),
        name="norm_route",
    )(x, mod, par, rw_pad)


def _weights_changed(te_ref, i):
    return (i == 0) | (te_ref[i] != te_ref[jnp.maximum(i - 1, 0)])


def _up_body(te_ref, nv_ref, a_ref, w1_ref, w3_ref, o_ref, w1b, w3b):
    i = pl.program_id(1)

    @pl.when(_weights_changed(te_ref, i))
    def _():
        w1b[...] = w1_ref[...].astype(BF16)
        w3b[...] = w3_ref[...].astype(BF16)

    @pl.when(i < nv_ref[0])
    def _():
        a = a_ref[...]
        h1 = jnp.dot(a, w1b[...], preferred_element_type=F32)
        h3 = jnp.dot(a, w3b[...], preferred_element_type=F32)
        o_ref[...] = (_silu(h1) * h3).astype(BF16)

    @pl.when(i >= nv_ref[0])
    def _():
        o_ref[...] = jnp.zeros_like(o_ref)


def _swiglu_up(tile_expert, n_valid, a, w1, w3, tm, bn_pref, name):
    m, k = a.shape
    f = w1.shape[2]
    bn = _pick(f, bn_pref, V7X_LANES)
    gs = pltpu.PrefetchScalarGridSpec(
        num_scalar_prefetch=2,
        grid=(f // bn, m // tm),
        in_specs=[pl.BlockSpec((tm, k), lambda j, i, te, nv: (i, 0)),
                  pl.BlockSpec((None, k, bn), lambda j, i, te, nv: (te[i], 0, j)),
                  pl.BlockSpec((None, k, bn), lambda j, i, te, nv: (te[i], 0, j))],
        out_specs=pl.BlockSpec((tm, bn), lambda j, i, te, nv: (i, j)),
        scratch_shapes=[pltpu.VMEM((k, bn), BF16), pltpu.VMEM((k, bn), BF16)],
    )
    return pl.pallas_call(
        _up_body,
        out_shape=jax.ShapeDtypeStruct((m, f), BF16),
        grid_spec=gs,
        compiler_params=_params(2),
        name=name,
    )(tile_expert, n_valid, a, w1, w3)


def _down_res_body(a_ref, w_ref, x_ref, mod_ref, o_ref, *, gate_row):
    y = jnp.dot(a_ref[...], w_ref[...].astype(BF16), preferred_element_type=F32)
    o_ref[...] = x_ref[...] + mod_ref[gate_row:gate_row + 1, :] * y


def _matmul_residual(a, w, xres, mod, gate_row, tm_pref, bn_pref, name):
    m, k = a.shape
    n = w.shape[1]
    b = mod.shape[0]
    s = m // b
    tm = _pick(s, tm_pref, 16)
    bn = _pick(n, bn_pref, V7X_LANES)
    return pl.pallas_call(
        functools.partial(_down_res_body, gate_row=gate_row),
        out_shape=jax.ShapeDtypeStruct((m, n), F32),
        grid=(m // tm, n // bn),
        in_specs=[pl.BlockSpec((tm, k), lambda i, j: (i, 0)),
                  pl.BlockSpec((k, bn), lambda i, j: (0, j)),
                  pl.BlockSpec((tm, bn), lambda i, j: (i, j)),
                  pl.BlockSpec((None, 8, bn), lambda i, j: ((i * tm) // s, 0, j))],
        out_specs=pl.BlockSpec((tm, bn), lambda i, j: (i, j)),
        compiler_params=_params(2),
        name=name,
    )(a, w, xres, mod)


def _down_grouped_body(te_ref, nv_ref, a_ref, w_ref, o_ref, wb):
    i = pl.program_id(1)

    @pl.when(_weights_changed(te_ref, i))
    def _():
        wb[...] = w_ref[...].astype(BF16)

    @pl.when(i < nv_ref[0])
    def _():
        o_ref[...] = jnp.dot(a_ref[...], wb[...], preferred_element_type=F32)

    @pl.when(i >= nv_ref[0])
    def _():
        o_ref[...] = jnp.zeros_like(o_ref)


def _down_grouped(tile_expert, n_valid, a, w, tm):
    m, k = a.shape
    n = w.shape[2]
    bn = _pick(n, 1024, V7X_LANES)
    gs = pltpu.PrefetchScalarGridSpec(
        num_scalar_prefetch=2,
        grid=(n // bn, m // tm),
        in_specs=[pl.BlockSpec((tm, k), lambda j, i, te, nv: (i, 0)),
                  pl.BlockSpec((None, k, bn), lambda j, i, te, nv: (te[i], 0, j))],
        out_specs=pl.BlockSpec((tm, bn), lambda j, i, te, nv: (i, j)),
        scratch_shapes=[pltpu.VMEM((k, bn), BF16)],
    )
    return pl.pallas_call(
        _down_grouped_body,
        out_shape=jax.ShapeDtypeStruct((m, n), F32),
        grid_spec=gs,
        compiler_params=_params(2),
        name="moe_down",
    )(tile_expert, n_valid, a, w)


def _proj_body(a_ref, w_ref, g_ref, *rest, dils, normed, out_scale, tm):
    o_refs, scr = rest[:len(dils)], rest[len(dils)]
    y = jnp.dot(a_ref[...], w_ref[...].astype(BF16), preferred_element_type=F32)
    n_slab = y.shape[1] // HEAD_DIM
    gvec = g_ref[...] * out_scale
    for s in range(n_slab):
        ys = y[:, s * HEAD_DIM:(s + 1) * HEAD_DIM]
        if normed:
            ms = jnp.mean(ys * ys, axis=-1, keepdims=True)
            ys = (ys * lax.rsqrt(ms + EPS)) * gvec
        scr[s] = ys
    for o_ref, d in zip(o_refs, dils):
        for s in range(n_slab):
            if d == 1:
                o_ref[s, 0] = scr[s].astype(BF16)
            else:
                for r in range(d):
                    o_ref[s, r] = scr[s, pl.ds(r, tm // d, stride=d), :].astype(BF16)


def _proj_heads(a, w, col0, gvec, dils, normed, out_scale, batch, name):
    m, k = a.shape
    s = m // batch
    d_model = k
    n_heads = d_model // HEAD_DIM
    tm = _pick(s, 1024, 16 * max(dils))
    bn = _pick(d_model, 512, HEAD_DIM)
    hpb = bn // HEAD_DIM
    tpb = s // tm
    cb0 = col0 // bn
    outs = tuple(jax.ShapeDtypeStruct((batch, n_heads, d, s // d, HEAD_DIM), BF16) for d in dils)
    out_specs = tuple(
        pl.BlockSpec((None, hpb, d, tm // d, HEAD_DIM), lambda i, j: (i // tpb, j, 0, i % tpb, 0)) for d in dils)
    return pl.pallas_call(
        functools.partial(_proj_body, dils=dils, normed=normed, out_scale=out_scale, tm=tm),
        out_shape=outs,
        grid=(m // tm, d_model // bn),
        in_specs=[pl.BlockSpec((tm, k), lambda i, j: (i, 0)),
                  pl.BlockSpec((k, bn), lambda i, j: (0, cb0 + j)),
                  pl.BlockSpec((1, HEAD_DIM), lambda i, j: (0, 0))],
        out_specs=out_specs,
        scratch_shapes=[pltpu.VMEM((hpb, tm, HEAD_DIM), F32)],
        compiler_params=_params(2),
        name=name,
    )(a, w, gvec)


def _band_buckets():
    i = np.arange(BLOCK)[:, None]
    kk = np.arange(2 * BLOCK)[None, :]
    rel = i + BLOCK - kk
    valid = (rel >= 0) & (rel <= N_TAPS)
    max_exact = NUM_BUCKETS // 2
    out = []
    for d in DILATIONS:
        n = np.maximum(rel, 0) * d
        nf = np.maximum(n, 1).astype(np.float32)
        large = max_exact + (np.log(nf / np.float32(max_exact)) / np.float32(math.log(MAX_DISTANCE / max_exact))
                             * np.float32(NUM_BUCKETS - max_exact)).astype(np.int32)
        bucket = np.where(n < max_exact, n, np.minimum(large, NUM_BUCKETS - 1))
        out.append(np.where(valid, bucket, -1))
    return np.stack(out).astype(np.int32)


def _bias_body(rb_ref, bk_ref, o_ref, *, n_heads):
    g = pl.program_id(0)
    h = pl.program_id(1)
    bucket = bk_ref[...]
    acc = jnp.full(bucket.shape, NEG_INF, F32)
    for b in range(NUM_BUCKETS):
        acc = jnp.where(bucket == b, rb_ref[(b * len(DILATIONS) + g) * n_heads + h], acc)
    o_ref[...] = acc


def _bias_bands(rel_bias, n_heads):
    buckets = jnp.asarray(_band_buckets())
    n_g = len(DILATIONS)
    return pl.pallas_call(
        functools.partial(_bias_body, n_heads=n_heads),
        out_shape=jax.ShapeDtypeStruct((n_g, n_heads, BLOCK, 2 * BLOCK), F32),
        grid=(n_g, n_heads),
        in_specs=[pl.BlockSpec(memory_space=pltpu.SMEM),
                  pl.BlockSpec((None, BLOCK, 2 * BLOCK), lambda g, h: (g, 0, 0))],
        out_specs=pl.BlockSpec((None, None, BLOCK, 2 * BLOCK), lambda g, h: (g, h, 0, 0)),
        compiler_params=_params(2),
        name="bias_bands",
    )(rel_bias.reshape(-1), buckets)


def _attn_body(*refs):
    n_g = len(DILATIONS)
    q_refs = refs[0:n_g]
    k_refs = refs[n_g:3 * n_g:2]
    kh_refs = refs[n_g + 1:3 * n_g:2]
    v_refs = refs[3 * n_g:5 * n_g:2]
    vh_refs = refs[3 * n_g + 1:5 * n_g:2]
    bias_ref = refs[5 * n_g]
    o_ref = refs[5 * n_g + 1]
    scr = refs[5 * n_g + 2:]
    kcats, vcats = scr[0:n_g], scr[n_g:2 * n_g]
    bsc, oscr, lscr = scr[2 * n_g], scr[2 * n_g + 1], scr[2 * n_g + 2]

    ti = pl.program_id(2)
    col = lax.broadcasted_iota(jnp.int32, (BLOCK, 2 * BLOCK), 1)
    pen = jnp.where((col < BLOCK) & (ti == 0), NEG_INF, 0.0)
    for g in range(n_g):
        band = bias_ref[g]
        bsc[0, g] = band
        bsc[1, g] = band + pen

    for g, d in enumerate(DILATIONS):
        n = ATTN_TILE // d
        nj = n // BLOCK
        kcat, vcat = kcats[g], vcats[g]
        kcat[:, 0:BLOCK, :] = kh_refs[g][...]
        kcat[:, BLOCK:, :] = k_refs[g][...]
        vcat[:, 0:BLOCK, :] = vh_refs[g][...]
        vcat[:, BLOCK:, :] = v_refs[g][...]
        q_ref = q_refs[g]

        def unit(u, carry, g=g, d=d, nj=nj, kcat=kcat, vcat=vcat, q_ref=q_ref):
            r = u // nj
            j = u % nj
            row = pl.multiple_of(j * BLOCK, BLOCK)
            q = q_ref[r, pl.ds(row, BLOCK), :]
            kk = kcat[r, pl.ds(row, 2 * BLOCK), :]
            vv = vcat[r, pl.ds(row, 2 * BLOCK), :]
            s = lax.dot_general(q, kk, (((1,), (1,)), ((), ())), preferred_element_type=F32)
            s = s + bsc[jnp.where(j == 0, 1, 0), g]
            m = jnp.max(s, axis=1, keepdims=True)
            p = jnp.exp(s - m)
            l = jnp.sum(p, axis=1, keepdims=True)
            acc = jnp.dot(p.astype(BF16), vv, preferred_element_type=F32)
            o = acc * (1.0 / l)
            lse = jnp.broadcast_to(m + jnp.log(l), (BLOCK, HEAD_DIM))
            start = j * (BLOCK * d) + r
            if d == 1:
                rows = pl.ds(pl.multiple_of(start, BLOCK), BLOCK)
            else:
                rows = pl.ds(start, BLOCK, stride=d)
            oscr[g, rows, :] = o
            lscr[g, rows, :] = lse
            return carry

        lax.fori_loop(0, d * nj, unit, 0, unroll=ATTN_UNROLL)

    chunk = 2 * BLOCK

    def merge(c, carry):
        rows = pl.ds(pl.multiple_of(c * chunk, chunk), chunk)
        ls = [lscr[g, rows, :] for g in range(n_g)]
        mx = functools.reduce(jnp.maximum, ls)
        es = [jnp.exp(l - mx) for l in ls]
        num = sum(e * oscr[g, rows, :] for g, e in enumerate(es))
        den = sum(es)
        o_ref[rows, :] = (num * (1.0 / den)).astype(BF16)
        return carry

    lax.fori_loop(0, ATTN_TILE // chunk, merge, 0)


def _attention(qs, ks, vs, bias, batch, seq, n_heads):
    n_g = len(DILATIONS)
    in_specs, args = [], []
    for g, d in enumerate(DILATIONS):
        n = ATTN_TILE // d
        in_specs.append(pl.BlockSpec((None, None, d, n, HEAD_DIM), lambda b, h, t: (b, h, 0, t, 0)))
        args.append(qs[g])
    for src in (ks, vs):
        for g, d in enumerate(DILATIONS):
            n = ATTN_TILE // d
            nb = n // BLOCK
            in_specs.append(pl.BlockSpec((None, None, d, n, HEAD_DIM), lambda b, h, t: (b, h, 0, t, 0)))
            in_specs.append(pl.BlockSpec((None, None, d, BLOCK, HEAD_DIM),
                                         lambda b, h, t, nb=nb: (b, h, 0, jnp.maximum(t * nb - 1, 0), 0)))
            args += [src[g], src[g]]
    in_specs.append(pl.BlockSpec((n_g, None, BLOCK, 2 * BLOCK), lambda b, h, t: (0, h, 0, 0)))
    args.append(bias)
    scratch = [pltpu.VMEM((d, BLOCK + ATTN_TILE // d, HEAD_DIM), BF16) for d in DILATIONS] * 2
    scratch += [pltpu.VMEM((2, n_g, BLOCK, 2 * BLOCK), F32),
                pltpu.VMEM((n_g, ATTN_TILE, HEAD_DIM), F32),
                pltpu.VMEM((n_g, ATTN_TILE, HEAD_DIM), F32)]
    return pl.pallas_call(
        _attn_body,
        out_shape=jax.ShapeDtypeStruct((batch, seq, n_heads * HEAD_DIM), BF16),
        grid=(batch, n_heads, seq // ATTN_TILE),
        in_specs=in_specs,
        out_specs=pl.BlockSpec((None, ATTN_TILE, HEAD_DIM), lambda b, h, t: (b, t, h)),
        scratch_shapes=scratch,
        compiler_params=_params(3),
        name="dilated_attn",
    )(*args)


def _gather_body(idx_ref, src_ref, o_ref, buf, sem, *, rb):
    def issue(r, c):
        pltpu.make_async_copy(src_ref.at[pl.ds(idx_ref[0, r], 1)], buf.at[pl.ds(r, 1)], sem).start()
        return c

    lax.fori_loop(0, rb, issue, 0, unroll=8)
    pltpu.make_async_copy(src_ref.at[pl.ds(0, rb)], buf, sem).wait()
    u = buf[...]
    half = u.shape[1]
    o_ref[:, :half] = pltpu.bitcast(u << 16, F32).astype(BF16)
    o_ref[:, half:] = pltpu.bitcast(u & jnp.uint32(0xFFFF0000), F32).astype(BF16)


def _gather_rows(src_idx, hp, rb):
    n_rows = src_idx.shape[0]
    half = hp.shape[1]
    idx3 = src_idx.reshape(n_rows // rb, 1, rb)
    return pl.pallas_call(
        functools.partial(_gather_body, rb=rb),
        out_shape=jax.ShapeDtypeStruct((n_rows, 2 * half), BF16),
        grid=(n_rows // rb,),
        in_specs=[pl.BlockSpec((None, 1, rb), lambda i: (i, 0, 0), memory_space=pltpu.SMEM),
                  pl.BlockSpec(memory_space=pl.ANY)],
        out_specs=pl.BlockSpec((rb, 2 * half), lambda i: (i, 0)),
        scratch_shapes=[pltpu.VMEM((rb, half), jnp.uint32), pltpu.SemaphoreType.DMA(())],
        compiler_params=_params(1),
        name="moe_gather",
    )(idx3, hp)


def _combine_body(pos_ref, y_ref, x_ref, route_ref, mod_ref, o_ref, buf, sem, *, tc, gate_row):
    def issue(r, c):
        pltpu.make_async_copy(y_ref.at[pl.ds(pos_ref[0, r], 1)], buf.at[0, pl.ds(r, 1)], sem).start()
        pltpu.make_async_copy(y_ref.at[pl.ds(pos_ref[0, tc + r], 1)], buf.at[1, pl.ds(r, 1)], sem).start()
        return c

    lax.fori_loop(0, tc, issue, 0, unroll=8)
    pltpu.make_async_copy(y_ref.at[pl.ds(0, tc)], buf.at[0], sem).wait()
    pltpu.make_async_copy(y_ref.at[pl.ds(0, tc)], buf.at[1], sem).wait()
    g1 = route_ref[:, 2:3]
    g2 = route_ref[:, 3:4]
    y = g1 * buf[0] + g2 * buf[1]
    o_ref[...] = x_ref[...] + mod_ref[gate_row:gate_row + 1, :] * y


def _combine(pos, y, x, route, mod, gate_row):
    m, d = x.shape
    b = mod.shape[0]
    s = m // b
    tc = pos.shape[2] // 2
    return pl.pallas_call(
        functools.partial(_combine_body, tc=tc, gate_row=gate_row),
        out_shape=jax.ShapeDtypeStruct((m, d), F32),
        grid=(m // tc,),
        in_specs=[pl.BlockSpec((None, 1, 2 * tc), lambda i: (i, 0, 0), memory_space=pltpu.SMEM),
                  pl.BlockSpec(memory_space=pl.ANY),
                  pl.BlockSpec((tc, d), lambda i: (i, 0)),
                  pl.BlockSpec((tc, V7X_LANES), lambda i: (i, 0)),
                  pl.BlockSpec((None, 8, d), lambda i: ((i * tc) // s, 0, 0))],
        out_specs=pl.BlockSpec((tc, d), lambda i: (i, 0)),
        scratch_shapes=[pltpu.VMEM((2, tc, d), F32), pltpu.SemaphoreType.DMA(())],
        compiler_params=_params(1),
        name="moe_combine",
    )(pos, y, x, route, mod)


def _rows8(*rows):
    b, d = rows[0].shape
    pad = [jnp.zeros((b, d), F32)] * (8 - len(rows))
    return jnp.stack(list(rows) + pad, axis=1)


def _par8(*rows):
    d = rows[0].shape[0]
    pad = [jnp.zeros((d,), F32)] * (8 - len(rows))
    return jnp.stack(list(rows) + pad, axis=0)


def kernel(x, c, ada_w, ada_b, norm_g, pool_w, pool_scale, kv_ada_w, kv_ada_b, kv_norm_g, w_k, w_v, k_norm_g,
           w_q, q_norm_g, w_o, rel_bias, ffn_w1, ffn_w3, ffn_w2, router_w, moe_w1, moe_w3, moe_w2):
    batch, seq, d = x.shape
    n_tok = batch * seq
    n_heads = d // HEAD_DIM
    n_exp = router_w.shape[2]
    assert d % HEAD_DIM == 0 and seq % ATTN_TILE == 0 and batch <= 8

    c8 = jnp.pad(c, ((0, 8 - batch), (0, 0)))
    mods = _ada(c8, ada_w.reshape(4, d, 3 * d), ada_b.reshape(4, 1, 3 * d))[:, :batch]
    kvm = _ada(c8, kv_ada_w[None], kv_ada_b[None, None])[0, :batch]
    sh = lambda m: (mods[m, :, :d], mods[m, :, d:2 * d], mods[m, :, 2 * d:])
    (s00, c00, g00), (s01, c01, g01), (s10, c10, g10), (s11, c11, g11) = sh(0), sh(1), sh(2), sh(3)
    kv_shift, kv_scale = kvm[:, :d], kvm[:, d:]

    x1, h2 = _mixer(x, _rows8(s00, c00, g00, s01, c01), _par8(norm_g[0, 0], norm_g[0, 1], pool_scale[0]),
                    pool_w[0].astype(BF16))
    tm_ffn = _pick(n_tok, 1024, 16)
    gact = _swiglu_up(jnp.zeros((n_tok // tm_ffn,), jnp.int32), jnp.full((1,), n_tok // tm_ffn, jnp.int32),
                      h2.reshape(n_tok, d), ffn_w1[:1], ffn_w3[:1], tm_ffn, 256, "ffn_up")
    x2 = _matmul_residual(gact, ffn_w2[0].astype(BF16), x1.reshape(n_tok, d), _rows8(g01), 0, 512, 512, "ffn_down")

    kvh, h3 = _norm2(x2.reshape(batch, seq, d), _rows8(kv_shift, kv_scale, s10, c10), _par8(kv_norm_g, norm_g[1, 0]))
    kvh, h3 = kvh.reshape(n_tok, d), h3.reshape(n_tok, d)
    ks = _proj_heads(kvh, w_k.astype(BF16), 0, k_norm_g.reshape(1, HEAD_DIM), DILATIONS, True, 1.0, batch, "proj_k")
    vs = _proj_heads(kvh, w_v.astype(BF16), 0, k_norm_g.reshape(1, HEAD_DIM), DILATIONS, False, 1.0, batch, "proj_v")
    wq = w_q[0].astype(BF16)
    qs = [_proj_heads(h3, wq, g * d, q_norm_g[0].reshape(1, HEAD_DIM), (dl,), True, HEAD_DIM ** -0.5, batch,
                      f"proj_q{dl}")[0] for g, dl in enumerate(DILATIONS)]

    bias = _bias_bands(rel_bias, n_heads)
    att = _attention(qs, ks, vs, bias, batch, seq, n_heads)
    x3 = _matmul_residual(att.reshape(n_tok, d), w_o[0].astype(BF16), x2, _rows8(g10), 0, 1024, 512, "attn_out")

    rw_pad = jnp.pad(router_w[0], ((0, 0), (0, V7X_LANES - n_exp)))
    hp, route = _route(x3.reshape(batch, seq, d), _rows8(s11, c11), _par8(norm_g[1, 1]), rw_pad, n_exp)
    hp, route = hp.reshape(n_tok, d // 2), route.reshape(n_tok, V7X_LANES)

    tm = _pick(n_tok, 512, 256)
    r_pad = TOP_K * n_tok + n_exp * tm
    e_flat = jnp.concatenate([route[:, 0], route[:, 1]]).astype(jnp.int32)
    onehot = (e_flat[:, None] == jnp.arange(n_exp, dtype=jnp.int32)[None, :]).astype(jnp.int32)
    csum = jnp.cumsum(onehot, axis=0)
    rank = jnp.sum((csum - 1) * onehot, axis=1)
    padded = ((csum[-1] + tm - 1) // tm) * tm
    gend = jnp.cumsum(padded)
    dest = (gend - padded)[e_flat] + rank
    tok = jnp.arange(TOP_K * n_tok, dtype=jnp.int32) % n_tok
    src_idx = jnp.zeros((r_pad,), jnp.int32).at[dest].set(tok)
    tile_expert = jnp.minimum(
        jnp.searchsorted(gend, jnp.arange(r_pad // tm, dtype=jnp.int32) * tm, side="right"), n_exp - 1
    ).astype(jnp.int32)

    n_valid = (gend[-1:] // tm).astype(jnp.int32)

    xg = _gather_rows(src_idx, hp, 256)
    hg = _swiglu_up(tile_expert, n_valid, xg, moe_w1[0], moe_w3[0], tm, 512, "moe_up")
    yg = _down_grouped(tile_expert, n_valid, hg, moe_w2[0], tm)

    tc = _pick(n_tok, 256, 8)
    pos = jnp.concatenate([dest[:n_tok].reshape(n_tok // tc, 1, tc), dest[n_tok:].reshape(n_tok // tc, 1, tc)], axis=2)
    out = _combine(pos.astype(jnp.int32), yg, x3, route, _rows8(g11), 0)
    return out.reshape(batch, seq, d)
```

```python
import functools
import math

import numpy as np
import jax
import jax.numpy as jnp
from jax import lax
from jax.experimental import pallas as pl
from jax.experimental.pallas import tpu as pltpu

F32 = jnp.float32
BF16 = jnp.bfloat16

V7X_LANES = 128
V7X_VMEM_LIMIT_BYTES = 58 * 1024 * 1024

HEAD_DIM = 128
BLOCK = 128
DILATIONS = (1, 4, 16)
N_TAPS = 128
NUM_BUCKETS = 32
MAX_DISTANCE = 2048
POOL_WINDOWS = (2, 4, 8, 16)
POOL_HALO = 16
TOP_K = 2
EPS = 1e-6
NEG_INF = -1e30
ATTN_TILE = BLOCK * DILATIONS[-1]
ATTN_UNROLL = 16


def _params(n_grid, vmem=V7X_VMEM_LIMIT_BYTES):
    return pltpu.CompilerParams(dimension_semantics=("arbitrary",) * n_grid, vmem_limit_bytes=vmem)


def _pick(n, pref, mult):
    if n <= pref:
        return n
    t = (pref // mult) * mult
    while t >= mult:
        if n % t == 0:
            return t
        t -= mult
    return n


def _silu(v):
    return v * jax.nn.sigmoid(v)


def _norm_mod(v, g, scale, shift):
    ms = jnp.mean(v * v, axis=-1, keepdims=True)
    return (v * lax.rsqrt(ms + EPS)) * g * (1.0 + scale) + shift


def _ada_body(c_ref, w_ref, b_ref, o_ref):
    cond = _silu(c_ref[...]).astype(BF16)
    o_ref[...] = jnp.dot(cond, w_ref[...].astype(BF16), preferred_element_type=F32) + b_ref[...]


def _ada(c8, w, b):
    m, d, n = w.shape
    bn = _pick(n, 512, V7X_LANES)
    return pl.pallas_call(
        _ada_body,
        out_shape=jax.ShapeDtypeStruct((m, 8, n), F32),
        grid=(m, n // bn),
        in_specs=[
            pl.BlockSpec((8, d), lambda i, j: (0, 0)),
            pl.BlockSpec((None, d, bn), lambda i, j: (i, 0, j)),
            pl.BlockSpec((None, 1, bn), lambda i, j: (i, 0, j)),
        ],
        out_specs=pl.BlockSpec((None, 8, bn), lambda i, j: (i, 0, j)),
        compiler_params=_params(2),
        name="ada",
    )(c8, w, b)


def _mixer_body(x_ref, halo_ref, mod_ref, par_ref, pw_ref, x1_ref, h2_ref, *, ts, group):
    i = pl.program_id(1)
    x = x_ref[...]
    g_a, g_f, pscale = par_ref[0:1, :], par_ref[1:2, :], par_ref[2:3, :]
    shift_a, scale_a, gate_a = mod_ref[0:1, :], mod_ref[1:2, :], mod_ref[2:3, :]
    shift_f, scale_f = mod_ref[3:4, :], mod_ref[4:5, :]

    h = _norm_mod(x, g_a, scale_a, shift_a)
    h_halo = jnp.where(i > 0, _norm_mod(halo_ref[...], g_a, scale_a, shift_a), 0.0)
    hh = jnp.concatenate([h_halo, h], axis=0)

    pos = i * ts + lax.broadcasted_iota(jnp.int32, (ts, 1), 0)
    ys = []
    for g, k in enumerate(POOL_WINDOWS):
        sl = slice(g * group, (g + 1) * group)
        a = hh[:, sl]
        span = 1
        while span < k:
            a = a[:-span] + a[span:]
            span *= 2
        start = POOL_HALO + 1 - k
        win = a[start:start + ts]
        cnt = jnp.minimum(pos + 1, k).astype(F32)
        p = win * (1.0 / cnt) - h[:, sl]
        ys.append(jnp.dot(p.astype(BF16), pw_ref[g], preferred_element_type=F32))
    y = jnp.concatenate(ys, axis=1) * pscale
    x1 = x + gate_a * y
    x1_ref[...] = x1
    h2_ref[...] = _norm_mod(x1, g_f, scale_f, shift_f).astype(BF16)


def _mixer(x, mod, par, pool_w):
    b, s, d = x.shape
    group = d // len(POOL_WINDOWS)
    ts = _pick(s, 256, POOL_HALO)
    hpb = ts // POOL_HALO
    return pl.pallas_call(
        functools.partial(_mixer_body, ts=ts, group=group),
        out_shape=(jax.ShapeDtypeStruct((b, s, d), F32), jax.ShapeDtypeStruct((b, s, d), BF16)),
        grid=(b, s // ts),
        in_specs=[
            pl.BlockSpec((None, ts, d), lambda bi, i: (bi, i, 0)),
            pl.BlockSpec((None, POOL_HALO, d), lambda bi, i: (bi, jnp.maximum(i * hpb - 1, 0), 0)),
            pl.BlockSpec((None, 8, d), lambda bi, i: (bi, 0, 0)),
            pl.BlockSpec((8, d), lambda bi, i: (0, 0)),
            pl.BlockSpec((len(POOL_WINDOWS), group, group), lambda bi, i: (0, 0, 0),
                         pipeline_mode=pl.Buffered(1)),
        ],
        out_specs=(pl.BlockSpec((None, ts, d), lambda bi, i: (bi, i, 0)),
                   pl.BlockSpec((None, ts, d), lambda bi, i: (bi, i, 0))),
        compiler_params=_params(2),
        name="mixer0",
    )(x, x, mod, par, pool_w)


def _norm2_body(x_ref, mod_ref, par_ref, a_ref, b_ref):
    x = x_ref[...]
    a_ref[...] = _norm_mod(x, par_ref[0:1, :], mod_ref[1:2, :], mod_ref[0:1, :]).astype(BF16)
    b_ref[...] = _norm_mod(x, par_ref[1:2, :], mod_ref[3:4, :], mod_ref[2:3, :]).astype(BF16)


def _norm2(x, mod, par):
    b, s, d = x.shape
    ts = _pick(s, 512, 16)
    row = pl.BlockSpec((None, ts, d), lambda bi, i: (bi, i, 0))
    return pl.pallas_call(
        _norm2_body,
        out_shape=(jax.ShapeDtypeStruct((b, s, d), BF16), jax.ShapeDtypeStruct((b, s, d), BF16)),
        grid=(b, s // ts),
        in_specs=[row, pl.BlockSpec((None, 8, d), lambda bi, i: (bi, 0, 0)),
                  pl.BlockSpec((8, d), lambda bi, i: (0, 0))],
        out_specs=(row, row),
        compiler_params=_params(2),
        name="norm_kv_q",
    )(x, mod, par)


def _route_body(x_ref, mod_ref, par_ref, rwh_ref, rwl_ref, hp_ref, route_ref, *, n_exp):
    x = x_ref[...]
    h = _norm_mod(x, par_ref[0:1, :], mod_ref[1:2, :], mod_ref[0:1, :])
    d = h.shape[1]
    bits = pltpu.bitcast(h.astype(BF16).astype(F32), jnp.uint32)
    packed = (bits[:, : d // 2] >> 16) | (bits[:, d // 2:] & jnp.uint32(0xFFFF0000))
    nseg = d // 2 // V7X_LANES
    ts = h.shape[0]
    for a in range(nseg):
        hp_ref[pl.ds(a, ts, stride=nseg), :] = packed[:, a * V7X_LANES:(a + 1) * V7X_LANES]
    h_hi = h.astype(BF16)
    h_lo = (h - h_hi.astype(F32)).astype(BF16)
    logits = (jnp.dot(h_hi, rwh_ref[...], preferred_element_type=F32)
              + jnp.dot(h_hi, rwl_ref[...], preferred_element_type=F32)
              + jnp.dot(h_lo, rwh_ref[...], preferred_element_type=F32))
    lane = lax.broadcasted_iota(jnp.int32, logits.shape, 1)
    lane_f = lane.astype(F32)
    lg = jnp.where(lane < n_exp, logits, -jnp.inf)
    v1 = jnp.max(lg, axis=1, keepdims=True)
    i1 = jnp.min(jnp.where(lg == v1, lane_f, float(V7X_LANES)), axis=1, keepdims=True)
    lg2 = jnp.where(lane_f == i1, -jnp.inf, lg)
    v2 = jnp.max(lg2, axis=1, keepdims=True)
    i2 = jnp.min(jnp.where(lg2 == v2, lane_f, float(V7X_LANES)), axis=1, keepdims=True)
    e2 = jnp.exp(v2 - v1)
    g1 = 1.0 / (1.0 + e2)
    g2 = e2 / (1.0 + e2)
    out = jnp.where(lane == 0, i1, 0.0)
    out = jnp.where(lane == 1, i2, out)
    out = jnp.where(lane == 2, g1, out)
    out = jnp.where(lane == 3, g2, out)
    route_ref[...] = out


def _route(x, mod, par, rw_pad, n_exp):
    b, s, d = x.shape
    rw_hi = rw_pad.astype(BF16)
    rw_hi_residual = (rw_pad - rw_hi.astype(F32)).astype(BF16)
    ts = _pick(s, 256, 16)
    nseg = d // 2 // V7X_LANES
    tpb = s // ts
    return pl.pallas_call(
        functools.partial(_route_body, n_exp=n_exp),
        out_shape=(jax.ShapeDtypeStruct((b * s * nseg, V7X_LANES), jnp.uint32),
                   jax.ShapeDtypeStruct((b, s, V7X_LANES), F32)),
        grid=(b, tpb),
        in_specs=[pl.BlockSpec((None, ts, d), lambda bi, i: (bi, i, 0)),
                  pl.BlockSpec((None, 8, d), lambda bi, i: (bi, 0, 0)),
                  pl.BlockSpec((8, d), lambda bi, i: (0, 0)),
                  pl.BlockSpec((d, V7X_LANES), lambda bi, i: (0, 0)),
                  pl.BlockSpec((d, V7X_LANES), lambda bi, i: (0, 0))],
        out_specs=(pl.BlockSpec((ts * nseg, V7X_LANES), lambda bi, i: (bi * tpb + i, 0)),
                   pl.BlockSpec((None, ts, V7X_LANES), lambda bi, i: (bi, i, 0))),
        compiler_params=_params(2),
        name="norm_route",
    )(x, mod, par, rw_hi, rw_hi_residual)


def _weights_changed(te_ref, i):
    return (i == 0) | (te_ref[i] != te_ref[jnp.maximum(i - 1, 0)])


def _up_body(te_ref, nv_ref, a_ref, w1_ref, w3_ref, o_ref, w1b, w3b):
    i = pl.program_id(1)

    @pl.when(_weights_changed(te_ref, i))
    def _():
        w1b[...] = w1_ref[...].astype(BF16)
        w3b[...] = w3_ref[...].astype(BF16)

    @pl.when(i < nv_ref[0])
    def _():
        a = a_ref[...]
        h1 = jnp.dot(a, w1b[...], preferred_element_type=F32)
        h3 = jnp.dot(a, w3b[...], preferred_element_type=F32)
        o_ref[...] = (_silu(h1) * h3).astype(BF16)

    @pl.when(i >= nv_ref[0])
    def _():
        o_ref[...] = jnp.zeros_like(o_ref)


def _swiglu_up(tile_expert, n_valid, a, w1, w3, tm, bn_pref, name):
    m, k = a.shape
    f = w1.shape[2]
    bn = _pick(f, bn_pref, V7X_LANES)
    gs = pltpu.PrefetchScalarGridSpec(
        num_scalar_prefetch=2,
        grid=(f // bn, m // tm),
        in_specs=[pl.BlockSpec((tm, k), lambda j, i, te, nv: (i, 0)),
                  pl.BlockSpec((None, k, bn), lambda j, i, te, nv: (te[i], 0, j)),
                  pl.BlockSpec((None, k, bn), lambda j, i, te, nv: (te[i], 0, j))],
        out_specs=pl.BlockSpec((tm, bn), lambda j, i, te, nv: (i, j)),
        scratch_shapes=[pltpu.VMEM((k, bn), BF16), pltpu.VMEM((k, bn), BF16)],
    )
    return pl.pallas_call(
        _up_body,
        out_shape=jax.ShapeDtypeStruct((m, f), BF16),
        grid_spec=gs,
        compiler_params=_params(2),
        name=name,
    )(tile_expert, n_valid, a, w1, w3)


def _down_res_body(a_ref, w_ref, x_ref, mod_ref, o_ref, *, gate_row):
    y = jnp.dot(a_ref[...], w_ref[...].astype(BF16), preferred_element_type=F32)
    o_ref[...] = x_ref[...] + mod_ref[gate_row:gate_row + 1, :] * y


def _matmul_residual(a, w, xres, mod, gate_row, tm_pref, bn_pref, name):
    m, k = a.shape
    n = w.shape[1]
    b = mod.shape[0]
    s = m // b
    tm = _pick(s, tm_pref, 16)
    bn = _pick(n, bn_pref, V7X_LANES)
    return pl.pallas_call(
        functools.partial(_down_res_body, gate_row=gate_row),
        out_shape=jax.ShapeDtypeStruct((m, n), F32),
        grid=(m // tm, n // bn),
        in_specs=[pl.BlockSpec((tm, k), lambda i, j: (i, 0)),
                  pl.BlockSpec((k, bn), lambda i, j: (0, j)),
                  pl.BlockSpec((tm, bn), lambda i, j: (i, j)),
                  pl.BlockSpec((None, 8, bn), lambda i, j: ((i * tm) // s, 0, j))],
        out_specs=pl.BlockSpec((tm, bn), lambda i, j: (i, j)),
        compiler_params=_params(2),
        name=name,
    )(a, w, xres, mod)


def _down_grouped_body(te_ref, nv_ref, a_ref, w_ref, o_ref, wb):
    i = pl.program_id(1)

    @pl.when(_weights_changed(te_ref, i))
    def _():
        wb[...] = w_ref[...].astype(BF16)

    @pl.when(i < nv_ref[0])
    def _():
        o_ref[...] = jnp.dot(a_ref[...], wb[...], preferred_element_type=F32)

    @pl.when(i >= nv_ref[0])
    def _():
        o_ref[...] = jnp.zeros_like(o_ref)


def _down_grouped(tile_expert, n_valid, a, w, tm):
    m, k = a.shape
    n = w.shape[2]
    bn = _pick(n, 1024, V7X_LANES)
    gs = pltpu.PrefetchScalarGridSpec(
        num_scalar_prefetch=2,
        grid=(n // bn, m // tm),
        in_specs=[pl.BlockSpec((tm, k), lambda j, i, te, nv: (i, 0)),
                  pl.BlockSpec((None, k, bn), lambda j, i, te, nv: (te[i], 0, j))],
        out_specs=pl.BlockSpec((tm, bn), lambda j, i, te, nv: (i, j)),
        scratch_shapes=[pltpu.VMEM((k, bn), BF16)],
    )
    return pl.pallas_call(
        _down_grouped_body,
        out_shape=jax.ShapeDtypeStruct((m, n), F32),
        grid_spec=gs,
        compiler_params=_params(2),
        name="moe_down",
    )(tile_expert, n_valid, a, w)


def _proj_body(a_ref, w_ref, g_ref, *rest, dils, normed, out_scale, tm, n_row_tiles):
    o_refs, (ybuf, wb) = rest[:len(dils)], rest[len(dils):]
    s = pl.program_id(0)
    n_slab = ybuf.shape[0]

    @pl.when(s == 0)
    def _():
        ybuf[...] = jnp.zeros_like(ybuf)

    @pl.when(s % n_row_tiles == 0)
    def _():
        wb[...] = w_ref[...].astype(BF16)

    gvec = g_ref[...] * out_scale

    def head_norm(ys):
        ms = jnp.mean(ys * ys, axis=-1, keepdims=True)
        return (ys * lax.rsqrt(ms + EPS)) * gvec

    for o_ref, d in zip(o_refs, dils):
        for sl in range(n_slab):
            for r in range(d):
                ys = ybuf[sl] if d == 1 else ybuf[sl, pl.ds(r, tm // d, stride=d), :]
                if normed:
                    ys = head_norm(ys)
                o_ref[sl, r] = ys.astype(BF16)

    y = jnp.dot(a_ref[...], wb[...], preferred_element_type=F32)
    for sl in range(n_slab):
        ybuf[sl] = y[:, sl * HEAD_DIM:(sl + 1) * HEAD_DIM]


def _proj_heads(a, w, col0, gvec, dils, normed, out_scale, batch, name):
    m, k = a.shape
    s = m // batch
    d_model = k
    n_heads = d_model // HEAD_DIM
    tm = _pick(s, 1024, 16 * max(dils))
    bn = _pick(d_model, 512, HEAD_DIM)
    hpb = bn // HEAD_DIM
    tpb = s // tm
    ni = m // tm
    n_steps = ni * (d_model // bn)
    cb0 = col0 // bn
    cur = lambda t: jnp.minimum(t, n_steps - 1)
    prev = lambda t: jnp.maximum(t - 1, 0)
    outs = tuple(jax.ShapeDtypeStruct((batch, n_heads, d, s // d, HEAD_DIM), BF16) for d in dils)
    out_specs = tuple(
        pl.BlockSpec((None, hpb, d, tm // d, HEAD_DIM),
                     lambda t: ((prev(t) % ni) // tpb, prev(t) // ni, 0, (prev(t) % ni) % tpb, 0)) for d in dils)
    return pl.pallas_call(
        functools.partial(_proj_body, dils=dils, normed=normed, out_scale=out_scale, tm=tm, n_row_tiles=ni),
        out_shape=outs,
        grid=(n_steps + 1,),
        in_specs=[pl.BlockSpec((tm, k), lambda t: (cur(t) % ni, 0)),
                  pl.BlockSpec((k, bn), lambda t: (0, cb0 + cur(t) // ni)),
                  pl.BlockSpec((1, HEAD_DIM), lambda t: (0, 0))],
        out_specs=out_specs,
        scratch_shapes=[pltpu.VMEM((hpb, tm, HEAD_DIM), F32), pltpu.VMEM((k, bn), BF16)],
        compiler_params=_params(1),
        name=name,
    )(a, w, gvec)


def _band_buckets():
    i = np.arange(BLOCK)[:, None]
    kk = np.arange(2 * BLOCK)[None, :]
    rel = i + BLOCK - kk
    valid = (rel >= 0) & (rel <= N_TAPS)
    max_exact = NUM_BUCKETS // 2
    out = []
    for d in DILATIONS:
        n = np.maximum(rel, 0) * d
        nf = np.maximum(n, 1).astype(np.float32)
        large = max_exact + (np.log(nf / np.float32(max_exact)) / np.float32(math.log(MAX_DISTANCE / max_exact))
                             * np.float32(NUM_BUCKETS - max_exact)).astype(np.int32)
        bucket = np.where(n < max_exact, n, np.minimum(large, NUM_BUCKETS - 1))
        out.append(np.where(valid, bucket, -1))
    return np.stack(out).astype(np.int32)


def _bias_body(rb_ref, bk_ref, o_ref, *, n_heads):
    g = pl.program_id(0)
    h = pl.program_id(1)
    bucket = bk_ref[...]
    acc = jnp.full(bucket.shape, NEG_INF, F32)
    for b in range(NUM_BUCKETS):
        acc = jnp.where(bucket == b, rb_ref[(b * len(DILATIONS) + g) * n_heads + h], acc)
    o_ref[...] = acc


def _bias_bands(rel_bias, n_heads):
    buckets = jnp.asarray(_band_buckets())
    n_g = len(DILATIONS)
    return pl.pallas_call(
        functools.partial(_bias_body, n_heads=n_heads),
        out_shape=jax.ShapeDtypeStruct((n_g, n_heads, BLOCK, 2 * BLOCK), F32),
        grid=(n_g, n_heads),
        in_specs=[pl.BlockSpec(memory_space=pltpu.SMEM),
                  pl.BlockSpec((None, BLOCK, 2 * BLOCK), lambda g, h: (g, 0, 0))],
        out_specs=pl.BlockSpec((None, None, BLOCK, 2 * BLOCK), lambda g, h: (g, h, 0, 0)),
        compiler_params=_params(2),
        name="bias_bands",
    )(rel_bias.reshape(-1), buckets)


def _attn_body(*refs):
    n_g = len(DILATIONS)
    q_refs = refs[0:n_g]
    k_refs = refs[n_g:3 * n_g:2]
    kh_refs = refs[n_g + 1:3 * n_g:2]
    v_refs = refs[3 * n_g:5 * n_g:2]
    vh_refs = refs[3 * n_g + 1:5 * n_g:2]
    bias_ref = refs[5 * n_g]
    o_ref = refs[5 * n_g + 1]
    scr = refs[5 * n_g + 2:]
    kcats, vcats = scr[0:n_g], scr[n_g:2 * n_g]
    bsc, oscr, lscr = scr[2 * n_g], scr[2 * n_g + 1], scr[2 * n_g + 2]

    ti = pl.program_id(2)
    col = lax.broadcasted_iota(jnp.int32, (BLOCK, 2 * BLOCK), 1)
    pen = jnp.where((col < BLOCK) & (ti == 0), NEG_INF, 0.0)
    for g in range(n_g):
        band = bias_ref[g]
        bsc[0, g] = band
        bsc[1, g] = band + pen

    for g, d in enumerate(DILATIONS):
        n = ATTN_TILE // d
        nj = n // BLOCK
        kcat, vcat = kcats[g], vcats[g]
        kcat[:, 0:BLOCK, :] = kh_refs[g][...]
        kcat[:, BLOCK:, :] = k_refs[g][...]
        vcat[:, 0:BLOCK, :] = vh_refs[g][...]
        vcat[:, BLOCK:, :] = v_refs[g][...]
        q_ref = q_refs[g]

        def unit(u, carry, g=g, d=d, nj=nj, kcat=kcat, vcat=vcat, q_ref=q_ref):
            r = u // nj
            j = u % nj
            row = pl.multiple_of(j * BLOCK, BLOCK)
            q = q_ref[r, pl.ds(row, BLOCK), :]
            kk = kcat[r, pl.ds(row, 2 * BLOCK), :]
            vv = vcat[r, pl.ds(row, 2 * BLOCK), :]
            s = lax.dot_general(q, kk, (((1,), (1,)), ((), ())), preferred_element_type=F32)
            s = s + bsc[jnp.where(j == 0, 1, 0), g]
            m = jnp.max(s, axis=1, keepdims=True)
            p = jnp.exp(s - m)
            l = jnp.sum(p, axis=1, keepdims=True)
            acc = jnp.dot(p.astype(BF16), vv, preferred_element_type=F32)
            o = acc * (1.0 / l)
            lse = jnp.broadcast_to(m + jnp.log(l), (BLOCK, HEAD_DIM))
            start = j * (BLOCK * d) + r
            if d == 1:
                rows = pl.ds(pl.multiple_of(start, BLOCK), BLOCK)
            else:
                rows = pl.ds(start, BLOCK, stride=d)
            oscr[g, rows, :] = o
            lscr[g, rows, :] = lse
            return carry

        lax.fori_loop(0, d * nj, unit, 0, unroll=ATTN_UNROLL)

    chunk = 2 * BLOCK

    def merge(c, carry):
        rows = pl.ds(pl.multiple_of(c * chunk, chunk), chunk)
        ls = [lscr[g, rows, :] for g in range(n_g)]
        mx = functools.reduce(jnp.maximum, ls)
        es = [jnp.exp(l - mx) for l in ls]
        num = sum(e * oscr[g, rows, :] for g, e in enumerate(es))
        den = sum(es)
        o_ref[rows, :] = (num * (1.0 / den)).astype(BF16)
        return carry

    lax.fori_loop(0, ATTN_TILE // chunk, merge, 0)


def _attention(qs, ks, vs, bias, batch, seq, n_heads):
    n_g = len(DILATIONS)
    in_specs, args = [], []
    for g, d in enumerate(DILATIONS):
        n = ATTN_TILE // d
        in_specs.append(pl.BlockSpec((None, None, d, n, HEAD_DIM), lambda b, h, t: (b, h, 0, t, 0)))
        args.append(qs[g])
    for src in (ks, vs):
        for g, d in enumerate(DILATIONS):
            n = ATTN_TILE // d
            nb = n // BLOCK
            in_specs.append(pl.BlockSpec((None, None, d, n, HEAD_DIM), lambda b, h, t: (b, h, 0, t, 0)))
            in_specs.append(pl.BlockSpec((None, None, d, BLOCK, HEAD_DIM),
                                         lambda b, h, t, nb=nb: (b, h, 0, jnp.maximum(t * nb - 1, 0), 0)))
            args += [src[g], src[g]]
    in_specs.append(pl.BlockSpec((n_g, None, BLOCK, 2 * BLOCK), lambda b, h, t: (0, h, 0, 0)))
    args.append(bias)
    scratch = [pltpu.VMEM((d, BLOCK + ATTN_TILE // d, HEAD_DIM), BF16) for d in DILATIONS] * 2
    scratch += [pltpu.VMEM((2, n_g, BLOCK, 2 * BLOCK), F32),
                pltpu.VMEM((n_g, ATTN_TILE, HEAD_DIM), F32),
                pltpu.VMEM((n_g, ATTN_TILE, HEAD_DIM), F32)]
    return pl.pallas_call(
        _attn_body,
        out_shape=jax.ShapeDtypeStruct((batch, seq, n_heads * HEAD_DIM), BF16),
        grid=(batch, n_heads, seq // ATTN_TILE),
        in_specs=in_specs,
        out_specs=pl.BlockSpec((None, ATTN_TILE, HEAD_DIM), lambda b, h, t: (b, t, h)),
        scratch_shapes=scratch,
        compiler_params=_params(3),
        name="dilated_attn",
    )(*args)


def _gather_body(nrows_ref, idx_ref, idxn_ref, src_ref, o_ref, buf, sem, *, rb, nseg):
    i = pl.program_id(0)
    n = pl.num_programs(0)
    holds_rows = lambda blk: blk * rb < nrows_ref[0]

    def start_block(ids_ref, slot):
        def issue(r, c):
            pltpu.make_async_copy(src_ref.at[pl.ds(ids_ref[0, r] * nseg, nseg)],
                                  buf.at[slot, pl.ds(r * nseg, nseg)], sem.at[slot]).start()
            return c
        lax.fori_loop(0, rb, issue, 0, unroll=8)

    @pl.when((i == 0) & holds_rows(0))
    def _():
        start_block(idx_ref, 0)

    @pl.when((i + 1 < n) & holds_rows(i + 1))
    def _():
        start_block(idxn_ref, (i + 1) % 2)

    half = nseg * V7X_LANES
    for slot in range(2):
        @pl.when(holds_rows(i) & (i % 2 == slot))
        def _(slot=slot):
            pltpu.make_async_copy(src_ref.at[pl.ds(0, rb * nseg)], buf.at[slot], sem.at[slot]).wait()
            for a in range(nseg):
                u = buf[slot, pl.ds(a, rb, stride=nseg), :]
                o_ref[:, a * V7X_LANES:(a + 1) * V7X_LANES] = pltpu.bitcast(u << 16, F32).astype(BF16)
                o_ref[:, half + a * V7X_LANES:half + (a + 1) * V7X_LANES] = (
                    pltpu.bitcast(u & jnp.uint32(0xFFFF0000), F32).astype(BF16))

    @pl.when(jnp.logical_not(holds_rows(i)))
    def _():
        o_ref[...] = jnp.zeros_like(o_ref)


def _gather_rows(n_rows_valid, src_idx, hp, d, rb):
    n_rows = src_idx.shape[0]
    nseg = d // 2 // V7X_LANES
    nblk = n_rows // rb
    idx3 = src_idx.reshape(nblk, 1, rb)
    gs = pltpu.PrefetchScalarGridSpec(
        num_scalar_prefetch=1,
        grid=(nblk,),
        in_specs=[pl.BlockSpec((None, 1, rb), lambda i, nr: (i, 0, 0), memory_space=pltpu.SMEM),
                  pl.BlockSpec((None, 1, rb), lambda i, nr: (jnp.minimum(i + 1, nblk - 1), 0, 0),
                               memory_space=pltpu.SMEM),
                  pl.BlockSpec(memory_space=pl.ANY)],
        out_specs=pl.BlockSpec((rb, d), lambda i, nr: (i, 0)),
        scratch_shapes=[pltpu.VMEM((2, rb * nseg, V7X_LANES), jnp.uint32), pltpu.SemaphoreType.DMA((2,))],
    )
    return pl.pallas_call(
        functools.partial(_gather_body, rb=rb, nseg=nseg),
        out_shape=jax.ShapeDtypeStruct((n_rows, d), BF16),
        grid_spec=gs,
        compiler_params=_params(1),
        name="moe_gather",
    )(n_rows_valid, idx3, idx3, hp)


def _combine_body(pos_ref, posn_ref, y_ref, x_ref, route_ref, mod_ref, o_ref, buf, sem, *, tc, gate_row):
    i = pl.program_id(0)
    n = pl.num_programs(0)

    def start_block(p_ref, slot):
        def issue(r, c):
            pltpu.make_async_copy(y_ref.at[pl.ds(p_ref[0, r], 1)], buf.at[slot, 0, pl.ds(r, 1)], sem.at[slot]).start()
            pltpu.make_async_copy(y_ref.at[pl.ds(p_ref[0, tc + r], 1)], buf.at[slot, 1, pl.ds(r, 1)],
                                  sem.at[slot]).start()
            return c
        lax.fori_loop(0, tc, issue, 0, unroll=8)

    @pl.when(i == 0)
    def _():
        start_block(pos_ref, 0)

    @pl.when(i + 1 < n)
    def _():
        start_block(posn_ref, (i + 1) % 2)

    g1 = route_ref[:, 2:3]
    g2 = route_ref[:, 3:4]
    for slot in range(2):
        @pl.when(i % 2 == slot)
        def _(slot=slot):
            pltpu.make_async_copy(y_ref.at[pl.ds(0, tc)], buf.at[slot, 0], sem.at[slot]).wait()
            pltpu.make_async_copy(y_ref.at[pl.ds(0, tc)], buf.at[slot, 1], sem.at[slot]).wait()
            y = g1 * buf[slot, 0] + g2 * buf[slot, 1]
            o_ref[...] = x_ref[...] + mod_ref[gate_row:gate_row + 1, :] * y


def _combine(pos, y, x, route, mod, gate_row):
    m, d = x.shape
    b = mod.shape[0]
    s = m // b
    tc = pos.shape[2] // 2
    nblk = m // tc
    return pl.pallas_call(
        functools.partial(_combine_body, tc=tc, gate_row=gate_row),
        out_shape=jax.ShapeDtypeStruct((m, d), F32),
        grid=(nblk,),
        in_specs=[pl.BlockSpec((None, 1, 2 * tc), lambda i: (i, 0, 0), memory_space=pltpu.SMEM),
                  pl.BlockSpec((None, 1, 2 * tc), lambda i: (jnp.minimum(i + 1, nblk - 1), 0, 0),
                               memory_space=pltpu.SMEM),
                  pl.BlockSpec(memory_space=pl.ANY),
                  pl.BlockSpec((tc, d), lambda i: (i, 0)),
                  pl.BlockSpec((tc, V7X_LANES), lambda i: (i, 0)),
                  pl.BlockSpec((None, 8, d), lambda i: ((i * tc) // s, 0, 0))],
        out_specs=pl.BlockSpec((tc, d), lambda i: (i, 0)),
        scratch_shapes=[pltpu.VMEM((2, 2, tc, d), F32), pltpu.SemaphoreType.DMA((2,))],
        compiler_params=_params(1),
        name="moe_combine",
    )(pos, pos, y, x, route, mod)


def _rows8(*rows):
    b, d = rows[0].shape
    pad = [jnp.zeros((b, d), F32)] * (8 - len(rows))
    return jnp.stack(list(rows) + pad, axis=1)


def _par8(*rows):
    d = rows[0].shape[0]
    pad = [jnp.zeros((d,), F32)] * (8 - len(rows))
    return jnp.stack(list(rows) + pad, axis=0)


def kernel(x, c, ada_w, ada_b, norm_g, pool_w, pool_scale, kv_ada_w, kv_ada_b, kv_norm_g, w_k, w_v, k_norm_g,
           w_q, q_norm_g, w_o, rel_bias, ffn_w1, ffn_w3, ffn_w2, router_w, moe_w1, moe_w3, moe_w2):
    batch, seq, d = x.shape
    n_tok = batch * seq
    n_heads = d // HEAD_DIM
    n_exp = router_w.shape[2]
    assert d % HEAD_DIM == 0 and seq % ATTN_TILE == 0 and batch <= 8

    c8 = jnp.pad(c, ((0, 8 - batch), (0, 0)))
    mods = _ada(c8, ada_w.reshape(4, d, 3 * d), ada_b.reshape(4, 1, 3 * d))[:, :batch]
    kvm = _ada(c8, kv_ada_w[None], kv_ada_b[None, None])[0, :batch]
    sh = lambda m: (mods[m, :, :d], mods[m, :, d:2 * d], mods[m, :, 2 * d:])
    (s00, c00, g00), (s01, c01, g01), (s10, c10, g10), (s11, c11, g11) = sh(0), sh(1), sh(2), sh(3)
    kv_shift, kv_scale = kvm[:, :d], kvm[:, d:]

    x1, h2 = _mixer(x, _rows8(s00, c00, g00, s01, c01), _par8(norm_g[0, 0], norm_g[0, 1], pool_scale[0]),
                    pool_w[0].astype(BF16))
    tm_ffn = _pick(n_tok, 1024, 16)
    gact = _swiglu_up(jnp.zeros((n_tok // tm_ffn,), jnp.int32), jnp.full((1,), n_tok // tm_ffn, jnp.int32),
                      h2.reshape(n_tok, d), ffn_w1[:1], ffn_w3[:1], tm_ffn, 256, "ffn_up")
    x2 = _matmul_residual(gact, ffn_w2[0].astype(BF16), x1.reshape(n_tok, d), _rows8(g01), 0, 512, 512, "ffn_down")

    kvh, h3 = _norm2(x2.reshape(batch, seq, d), _rows8(kv_shift, kv_scale, s10, c10), _par8(kv_norm_g, norm_g[1, 0]))
    kvh, h3 = kvh.reshape(n_tok, d), h3.reshape(n_tok, d)
    ks = _proj_heads(kvh, w_k, 0, k_norm_g.reshape(1, HEAD_DIM), DILATIONS, True, 1.0, batch, "proj_k")
    vs = _proj_heads(kvh, w_v, 0, k_norm_g.reshape(1, HEAD_DIM), DILATIONS, False, 1.0, batch, "proj_v")
    qs = [_proj_heads(h3, w_q[0], g * d, q_norm_g[0].reshape(1, HEAD_DIM), (dl,), True, HEAD_DIM ** -0.5, batch,
                      f"proj_q{dl}")[0] for g, dl in enumerate(DILATIONS)]

    bias = _bias_bands(rel_bias, n_heads)
    att = _attention(qs, ks, vs, bias, batch, seq, n_heads)
    x3 = _matmul_residual(att.reshape(n_tok, d), w_o[0].astype(BF16), x2, _rows8(g10), 0, 1024, 512, "attn_out")

    rw_pad = jnp.pad(router_w[0], ((0, 0), (0, V7X_LANES - n_exp)))
    hp, route = _route(x3.reshape(batch, seq, d), _rows8(s11, c11), _par8(norm_g[1, 1]), rw_pad, n_exp)
    route = route.reshape(n_tok, V7X_LANES)

    tm = _pick(n_tok, 512, 256)
    r_pad = TOP_K * n_tok + n_exp * tm
    e_flat = jnp.concatenate([route[:, 0], route[:, 1]]).astype(jnp.int32)
    onehot = (e_flat[:, None] == jnp.arange(n_exp, dtype=jnp.int32)[None, :]).astype(jnp.int32)
    csum = jnp.cumsum(onehot, axis=0)
    rank = jnp.sum((csum - 1) * onehot, axis=1)
    padded = ((csum[-1] + tm - 1) // tm) * tm
    gend = jnp.cumsum(padded)
    dest = (gend - padded)[e_flat] + rank
    tok = jnp.arange(TOP_K * n_tok, dtype=jnp.int32) % n_tok
    src_idx = jnp.zeros((r_pad,), jnp.int32).at[dest].set(tok)
    tile_expert = jnp.minimum(
        jnp.searchsorted(gend, jnp.arange(r_pad // tm, dtype=jnp.int32) * tm, side="right"), n_exp - 1
    ).astype(jnp.int32)

    n_valid = (gend[-1:] // tm).astype(jnp.int32)

    xg = _gather_rows(gend[-1:].astype(jnp.int32), src_idx, hp, d, 256)
    hg = _swiglu_up(tile_expert, n_valid, xg, moe_w1[0], moe_w3[0], tm, 512, "moe_up")
    yg = _down_grouped(tile_expert, n_valid, hg, moe_w2[0], tm)

    tc = _pick(n_tok, 256, 8)
    pos = jnp.concatenate([dest[:n_tok].reshape(n_tok // tc, 1, tc), dest[n_tok:].reshape(n_tok // tc, 1, tc)], axis=2)
    out = _combine(pos.astype(jnp.int32), yg, x3, route, _rows8(g11), 0)
    return out.reshape(batch, seq, d)
```

```python
import functools
import math

import numpy as np
import jax
import jax.numpy as jnp
from jax import lax
from jax.experimental import pallas as pl
from jax.experimental.pallas import tpu as pltpu

F32 = jnp.float32
BF16 = jnp.bfloat16

V7X_LANES = 128
V7X_VMEM_LIMIT_BYTES = 60 * 1024 * 1024

HEAD_DIM = 128
BLOCK = 128
DILATIONS = (1, 4, 16)
N_TAPS = 128
NUM_BUCKETS = 32
MAX_DISTANCE = 2048
POOL_WINDOWS = (2, 4, 8, 16)
POOL_HALO = 16
TOP_K = 2
EPS = 1e-6
NEG_INF = -1e30
ATTN_TILE = BLOCK * DILATIONS[-1]
ATTN_UNROLL = 16
ATTN_HEADS = 2


def _params(n_grid, vmem=V7X_VMEM_LIMIT_BYTES):
    return pltpu.CompilerParams(dimension_semantics=("arbitrary",) * n_grid, vmem_limit_bytes=vmem)


def _pick(n, pref, mult):
    if n <= pref:
        return n
    t = (pref // mult) * mult
    while t >= mult:
        if n % t == 0:
            return t
        t -= mult
    return n


def _silu(v):
    return v * jax.nn.sigmoid(v)


def _norm_mod(v, g, scale, shift):
    ms = jnp.mean(v * v, axis=-1, keepdims=True)
    return (v * lax.rsqrt(ms + EPS)) * (g * (1.0 + scale)) + shift


def _ada_body(c_ref, w_ref, b_ref, o_ref):
    cond = _silu(c_ref[...]).astype(BF16)
    o_ref[...] = jnp.dot(cond, w_ref[...].astype(BF16), preferred_element_type=F32) + b_ref[...]


def _ada(c8, w, b):
    m, d, n = w.shape
    bn = _pick(n, 512, V7X_LANES)
    return pl.pallas_call(
        _ada_body,
        out_shape=jax.ShapeDtypeStruct((m, 8, n), F32),
        grid=(m, n // bn),
        in_specs=[
            pl.BlockSpec((8, d), lambda i, j: (0, 0)),
            pl.BlockSpec((None, d, bn), lambda i, j: (i, 0, j)),
            pl.BlockSpec((None, 1, bn), lambda i, j: (i, 0, j)),
        ],
        out_specs=pl.BlockSpec((None, 8, bn), lambda i, j: (i, 0, j)),
        compiler_params=_params(2),
        name="ada",
    )(c8, w, b)


def _mixer_body(x_ref, halo_ref, mod_ref, par_ref, pw_ref, x1_ref, h2_ref, *, ts, group):
    i = pl.program_id(1)
    x = x_ref[...]
    g_a, g_f, pscale = par_ref[0:1, :], par_ref[1:2, :], par_ref[2:3, :]
    shift_a, scale_a, gate_a = mod_ref[0:1, :], mod_ref[1:2, :], mod_ref[2:3, :]
    shift_f, scale_f = mod_ref[3:4, :], mod_ref[4:5, :]

    h = _norm_mod(x, g_a, scale_a, shift_a)
    h_halo = jnp.where(i > 0, _norm_mod(halo_ref[...], g_a, scale_a, shift_a), 0.0)
    hh = jnp.concatenate([h_halo, h], axis=0)

    lt = min(V7X_LANES, group)
    pos = i * ts + lax.broadcasted_iota(jnp.int32, (ts, lt), 0)
    ys = []
    for g, k in enumerate(POOL_WINDOWS):
        sl = slice(g * group, (g + 1) * group)
        a = hh[:, sl]
        span = 1
        while span < k:
            a = a[:-span] + a[span:]
            span *= 2
        start = POOL_HALO + 1 - k
        win = a[start:start + ts]
        inv_cnt = 1.0 / jnp.minimum(pos + 1, k).astype(F32)
        hg = h[:, sl]
        p = jnp.concatenate([win[:, c:c + lt] * inv_cnt - hg[:, c:c + lt] for c in range(0, group, lt)], axis=1)
        ys.append(jnp.dot(p.astype(BF16), pw_ref[g], preferred_element_type=F32))
    y = jnp.concatenate(ys, axis=1) * pscale
    x1 = x + gate_a * y
    x1_ref[...] = x1
    h2_ref[...] = _norm_mod(x1, g_f, scale_f, shift_f).astype(BF16)


def _mixer(x, mod, par, pool_w):
    b, s, d = x.shape
    group = d // len(POOL_WINDOWS)
    ts = _pick(s, 256, POOL_HALO)
    hpb = ts // POOL_HALO
    return pl.pallas_call(
        functools.partial(_mixer_body, ts=ts, group=group),
        out_shape=(jax.ShapeDtypeStruct((b, s, d), F32), jax.ShapeDtypeStruct((b, s, d), BF16)),
        grid=(b, s // ts),
        in_specs=[
            pl.BlockSpec((None, ts, d), lambda bi, i: (bi, i, 0)),
            pl.BlockSpec((None, POOL_HALO, d), lambda bi, i: (bi, jnp.maximum(i * hpb - 1, 0), 0)),
            pl.BlockSpec((None, 8, d), lambda bi, i: (bi, 0, 0)),
            pl.BlockSpec((8, d), lambda bi, i: (0, 0)),
            pl.BlockSpec((len(POOL_WINDOWS), group, group), lambda bi, i: (0, 0, 0),
                         pipeline_mode=pl.Buffered(1)),
        ],
        out_specs=(pl.BlockSpec((None, ts, d), lambda bi, i: (bi, i, 0)),
                   pl.BlockSpec((None, ts, d), lambda bi, i: (bi, i, 0))),
        compiler_params=_params(2),
        name="mixer0",
    )(x, x, mod, par, pool_w)


def _norm2_body(x_ref, mod_ref, par_ref, a_ref, b_ref):
    x = x_ref[...]
    a_ref[...] = _norm_mod(x, par_ref[0:1, :], mod_ref[1:2, :], mod_ref[0:1, :]).astype(BF16)
    b_ref[...] = _norm_mod(x, par_ref[1:2, :], mod_ref[3:4, :], mod_ref[2:3, :]).astype(BF16)


def _norm2(x, mod, par):
    b, s, d = x.shape
    ts = _pick(s, 512, 16)
    row = pl.BlockSpec((None, ts, d), lambda bi, i: (bi, i, 0))
    return pl.pallas_call(
        _norm2_body,
        out_shape=(jax.ShapeDtypeStruct((b, s, d), BF16), jax.ShapeDtypeStruct((b, s, d), BF16)),
        grid=(b, s // ts),
        in_specs=[row, pl.BlockSpec((None, 8, d), lambda bi, i: (bi, 0, 0)),
                  pl.BlockSpec((8, d), lambda bi, i: (0, 0))],
        out_specs=(row, row),
        compiler_params=_params(2),
        name="norm_kv_q",
    )(x, mod, par)


def _route_body(x_ref, mod_ref, par_ref, rwh_ref, rwl_ref, hp_ref, route_ref, *, n_exp):
    x = x_ref[...]
    h = _norm_mod(x, par_ref[0:1, :], mod_ref[1:2, :], mod_ref[0:1, :])
    d = h.shape[1]
    bits = pltpu.bitcast(h.astype(BF16).astype(F32), jnp.uint32)
    packed = (bits[:, : d // 2] >> 16) | (bits[:, d // 2:] & jnp.uint32(0xFFFF0000))
    nseg = d // 2 // V7X_LANES
    ts = h.shape[0]
    for a in range(nseg):
        hp_ref[pl.ds(a, ts, stride=nseg), :] = packed[:, a * V7X_LANES:(a + 1) * V7X_LANES]
    h_hi = h.astype(BF16)
    h_lo = (h - h_hi.astype(F32)).astype(BF16)
    logits = (jnp.dot(h_hi, rwh_ref[...], preferred_element_type=F32)
              + jnp.dot(h_hi, rwl_ref[...], preferred_element_type=F32)
              + jnp.dot(h_lo, rwh_ref[...], preferred_element_type=F32))
    lane = lax.broadcasted_iota(jnp.int32, logits.shape, 1)
    lane_f = lane.astype(F32)
    lg = jnp.where(lane < n_exp, logits, -jnp.inf)
    v1 = jnp.max(lg, axis=1, keepdims=True)
    i1 = jnp.min(jnp.where(lg == v1, lane_f, float(V7X_LANES)), axis=1, keepdims=True)
    lg2 = jnp.where(lane_f == i1, -jnp.inf, lg)
    v2 = jnp.max(lg2, axis=1, keepdims=True)
    i2 = jnp.min(jnp.where(lg2 == v2, lane_f, float(V7X_LANES)), axis=1, keepdims=True)
    e2 = jnp.exp(v2 - v1)
    g1 = 1.0 / (1.0 + e2)
    g2 = e2 / (1.0 + e2)
    out = jnp.where(lane == 0, i1, 0.0)
    out = jnp.where(lane == 1, i2, out)
    out = jnp.where(lane == 2, g1, out)
    out = jnp.where(lane == 3, g2, out)
    route_ref[...] = out


def _route(x, mod, par, rw_pad, n_exp):
    b, s, d = x.shape
    rw_hi = rw_pad.astype(BF16)
    rw_hi_residual = (rw_pad - rw_hi.astype(F32)).astype(BF16)
    ts = _pick(s, 256, 16)
    nseg = d // 2 // V7X_LANES
    tpb = s // ts
    return pl.pallas_call(
        functools.partial(_route_body, n_exp=n_exp),
        out_shape=(jax.ShapeDtypeStruct((b * s * nseg, V7X_LANES), jnp.uint32),
                   jax.ShapeDtypeStruct((b, s, V7X_LANES), F32)),
        grid=(b, tpb),
        in_specs=[pl.BlockSpec((None, ts, d), lambda bi, i: (bi, i, 0)),
                  pl.BlockSpec((None, 8, d), lambda bi, i: (bi, 0, 0)),
                  pl.BlockSpec((8, d), lambda bi, i: (0, 0)),
                  pl.BlockSpec((d, V7X_LANES), lambda bi, i: (0, 0)),
                  pl.BlockSpec((d, V7X_LANES), lambda bi, i: (0, 0))],
        out_specs=(pl.BlockSpec((ts * nseg, V7X_LANES), lambda bi, i: (bi * tpb + i, 0)),
                   pl.BlockSpec((None, ts, V7X_LANES), lambda bi, i: (bi, i, 0))),
        compiler_params=_params(2),
        name="norm_route",
    )(x, mod, par, rw_hi, rw_hi_residual)


def _weights_changed(te_ref, i):
    return (i == 0) | (te_ref[i] != te_ref[jnp.maximum(i - 1, 0)])


def _up_body(te_ref, nv_ref, a_ref, w1_ref, w3_ref, o_ref, w1b, w3b):
    i = pl.program_id(1)

    @pl.when(_weights_changed(te_ref, i))
    def _():
        w1b[...] = w1_ref[...].astype(BF16)
        w3b[...] = w3_ref[...].astype(BF16)

    @pl.when(i < nv_ref[0])
    def _():
        a = a_ref[...]
        h1 = jnp.dot(a, w1b[...], preferred_element_type=F32)
        h3 = jnp.dot(a, w3b[...], preferred_element_type=F32)
        o_ref[...] = (_silu(h1) * h3).astype(BF16)

    @pl.when(i >= nv_ref[0])
    def _():
        o_ref[...] = jnp.zeros_like(o_ref)


def _swiglu_up(tile_expert, n_valid, a, w1, w3, tm, bn_pref, name):
    m, k = a.shape
    f = w1.shape[2]
    bn = _pick(f, bn_pref, V7X_LANES)
    gs = pltpu.PrefetchScalarGridSpec(
        num_scalar_prefetch=2,
        grid=(f // bn, m // tm),
        in_specs=[pl.BlockSpec((tm, k), lambda j, i, te, nv: (i, 0)),
                  pl.BlockSpec((None, k, bn), lambda j, i, te, nv: (te[i], 0, j)),
                  pl.BlockSpec((None, k, bn), lambda j, i, te, nv: (te[i], 0, j))],
        out_specs=pl.BlockSpec((tm, bn), lambda j, i, te, nv: (i, j)),
        scratch_shapes=[pltpu.VMEM((k, bn), BF16), pltpu.VMEM((k, bn), BF16)],
    )
    return pl.pallas_call(
        _up_body,
        out_shape=jax.ShapeDtypeStruct((m, f), BF16),
        grid_spec=gs,
        compiler_params=_params(2),
        name=name,
    )(tile_expert, n_valid, a, w1, w3)


def _down_res_body(a_ref, w_ref, x_ref, mod_ref, o_ref, *, gate_row):
    y = jnp.dot(a_ref[...], w_ref[...].astype(BF16), preferred_element_type=F32)
    o_ref[...] = x_ref[...] + mod_ref[gate_row:gate_row + 1, :] * y


def _matmul_residual(a, w, xres, mod, gate_row, tm_pref, bn_pref, name):
    m, k = a.shape
    n = w.shape[1]
    b = mod.shape[0]
    s = m // b
    tm = _pick(s, tm_pref, 16)
    bn = _pick(n, bn_pref, V7X_LANES)
    return pl.pallas_call(
        functools.partial(_down_res_body, gate_row=gate_row),
        out_shape=jax.ShapeDtypeStruct((m, n), F32),
        grid=(m // tm, n // bn),
        in_specs=[pl.BlockSpec((tm, k), lambda i, j: (i, 0)),
                  pl.BlockSpec((k, bn), lambda i, j: (0, j)),
                  pl.BlockSpec((tm, bn), lambda i, j: (i, j)),
                  pl.BlockSpec((None, 8, bn), lambda i, j: ((i * tm) // s, 0, j))],
        out_specs=pl.BlockSpec((tm, bn), lambda i, j: (i, j)),
        compiler_params=_params(2),
        name=name,
    )(a, w, xres, mod)


def _down_grouped_body(te_ref, nv_ref, a_ref, w_ref, o_ref, wb):
    i = pl.program_id(1)

    @pl.when(_weights_changed(te_ref, i))
    def _():
        wb[...] = w_ref[...].astype(BF16)

    @pl.when(i < nv_ref[0])
    def _():
        o_ref[...] = jnp.dot(a_ref[...], wb[...], preferred_element_type=F32)

    @pl.when(i >= nv_ref[0])
    def _():
        o_ref[...] = jnp.zeros_like(o_ref)


def _down_grouped(tile_expert, n_valid, a, w, tm):
    m, k = a.shape
    n = w.shape[2]
    bn = _pick(n, 1024, V7X_LANES)
    gs = pltpu.PrefetchScalarGridSpec(
        num_scalar_prefetch=2,
        grid=(n // bn, m // tm),
        in_specs=[pl.BlockSpec((tm, k), lambda j, i, te, nv: (i, 0)),
                  pl.BlockSpec((None, k, bn), lambda j, i, te, nv: (te[i], 0, j))],
        out_specs=pl.BlockSpec((tm, bn), lambda j, i, te, nv: (i, j)),
        scratch_shapes=[pltpu.VMEM((k, bn), BF16)],
    )
    return pl.pallas_call(
        _down_grouped_body,
        out_shape=jax.ShapeDtypeStruct((m, n), F32),
        grid_spec=gs,
        compiler_params=_params(2),
        name="moe_down",
    )(tile_expert, n_valid, a, w)


def _proj_body(a_ref, w_ref, g_ref, *rest, dils, normed, out_scale, tm, n_row_tiles):
    o_refs, (ybuf, wb) = rest[:len(dils)], rest[len(dils):]
    s = pl.program_id(0)
    n_slab = ybuf.shape[0]

    @pl.when(s == 0)
    def _():
        ybuf[...] = jnp.zeros_like(ybuf)

    @pl.when(s % n_row_tiles == 0)
    def _():
        wb[...] = w_ref[...].astype(BF16)

    gvec = g_ref[...] * out_scale

    def head_norm(ys):
        ms = jnp.mean(ys * ys, axis=-1, keepdims=True)
        return (ys * lax.rsqrt(ms + EPS)) * gvec

    for o_ref, d in zip(o_refs, dils):
        for sl in range(n_slab):
            for r in range(d):
                ys = ybuf[sl] if d == 1 else ybuf[sl, pl.ds(r, tm // d, stride=d), :]
                if normed:
                    ys = head_norm(ys)
                o_ref[sl, r] = ys.astype(BF16)

    y = jnp.dot(a_ref[...], wb[...], preferred_element_type=F32)
    for sl in range(n_slab):
        ybuf[sl] = y[:, sl * HEAD_DIM:(sl + 1) * HEAD_DIM]


def _proj_heads(a, w, col0, gvec, dils, normed, out_scale, batch, name):
    m, k = a.shape
    s = m // batch
    d_model = k
    n_heads = d_model // HEAD_DIM
    tm = _pick(s, 1024, 16 * max(dils))
    bn = _pick(d_model, 512, HEAD_DIM)
    hpb = bn // HEAD_DIM
    tpb = s // tm
    ni = m // tm
    n_steps = ni * (d_model // bn)
    cb0 = col0 // bn
    cur = lambda t: jnp.minimum(t, n_steps - 1)
    prev = lambda t: jnp.maximum(t - 1, 0)
    outs = tuple(jax.ShapeDtypeStruct((batch, n_heads, d, s // d, HEAD_DIM), BF16) for d in dils)
    out_specs = tuple(
        pl.BlockSpec((None, hpb, d, tm // d, HEAD_DIM),
                     lambda t: ((prev(t) % ni) // tpb, prev(t) // ni, 0, (prev(t) % ni) % tpb, 0)) for d in dils)
    return pl.pallas_call(
        functools.partial(_proj_body, dils=dils, normed=normed, out_scale=out_scale, tm=tm, n_row_tiles=ni),
        out_shape=outs,
        grid=(n_steps + 1,),
        in_specs=[pl.BlockSpec((tm, k), lambda t: (cur(t) % ni, 0)),
                  pl.BlockSpec((k, bn), lambda t: (0, cb0 + cur(t) // ni)),
                  pl.BlockSpec((1, HEAD_DIM), lambda t: (0, 0))],
        out_specs=out_specs,
        scratch_shapes=[pltpu.VMEM((hpb, tm, HEAD_DIM), F32), pltpu.VMEM((k, bn), BF16)],
        compiler_params=_params(1),
        name=name,
    )(a, w, gvec)


def _band_buckets():
    i = np.arange(BLOCK)[:, None]
    kk = np.arange(2 * BLOCK)[None, :]
    rel = i + BLOCK - kk
    valid = (rel >= 0) & (rel <= N_TAPS)
    max_exact = NUM_BUCKETS // 2
    out = []
    for d in DILATIONS:
        n = np.maximum(rel, 0) * d
        nf = np.maximum(n, 1).astype(np.float32)
        large = max_exact + (np.log(nf / np.float32(max_exact)) / np.float32(math.log(MAX_DISTANCE / max_exact))
                             * np.float32(NUM_BUCKETS - max_exact)).astype(np.int32)
        bucket = np.where(n < max_exact, n, np.minimum(large, NUM_BUCKETS - 1))
        out.append(np.where(valid, bucket, -1))
    return np.stack(out).astype(np.int32)


def _bias_body(rb_ref, bk_ref, o_ref, *, n_heads):
    g = pl.program_id(0)
    h = pl.program_id(1)
    bucket = bk_ref[...]
    acc = jnp.full(bucket.shape, NEG_INF, F32)
    for b in range(NUM_BUCKETS):
        acc = jnp.where(bucket == b, rb_ref[(b * len(DILATIONS) + g) * n_heads + h], acc)
    o_ref[...] = acc


def _bias_bands(rel_bias, n_heads):
    buckets = jnp.asarray(_band_buckets())
    n_g = len(DILATIONS)
    return pl.pallas_call(
        functools.partial(_bias_body, n_heads=n_heads),
        out_shape=jax.ShapeDtypeStruct((n_g, n_heads, BLOCK, 2 * BLOCK), F32),
        grid=(n_g, n_heads),
        in_specs=[pl.BlockSpec(memory_space=pltpu.SMEM),
                  pl.BlockSpec((None, BLOCK, 2 * BLOCK), lambda g, h: (g, 0, 0))],
        out_specs=pl.BlockSpec((None, None, BLOCK, 2 * BLOCK), lambda g, h: (g, h, 0, 0)),
        compiler_params=_params(2),
        name="bias_bands",
    )(rel_bias.reshape(-1), buckets)


def _attn_body(*refs, hb):
    n_g = len(DILATIONS)
    q_refs = refs[0:n_g]
    k_refs = refs[n_g:3 * n_g:2]
    kh_refs = refs[n_g + 1:3 * n_g:2]
    v_refs = refs[3 * n_g:5 * n_g:2]
    vh_refs = refs[3 * n_g + 1:5 * n_g:2]
    bias_ref = refs[5 * n_g]
    o_ref = refs[5 * n_g + 1]
    scr = refs[5 * n_g + 2:]
    kcats, vcats = scr[0:n_g], scr[n_g:2 * n_g]
    bsc, oscr, mscr, lscr = scr[2 * n_g:2 * n_g + 4]

    ti = pl.program_id(2)
    col = lax.broadcasted_iota(jnp.int32, (BLOCK, 2 * BLOCK), 1)
    pen = jnp.where((col < BLOCK) & (ti == 0), NEG_INF, 0.0)

    for h in range(hb):
        for g in range(n_g):
            band = bias_ref[g, h]
            bsc[0, g] = band
            bsc[1, g] = band + pen

        for g, d in enumerate(DILATIONS):
            n = ATTN_TILE // d
            nj = n // BLOCK
            kcat, vcat = kcats[g], vcats[g]
            kcat[:, 0:BLOCK, :] = kh_refs[g][h]
            kcat[:, BLOCK:, :] = k_refs[g][h]
            vcat[:, 0:BLOCK, :] = vh_refs[g][h]
            vcat[:, BLOCK:, :] = v_refs[g][h]
            q_ref = q_refs[g]

            def unit(u, carry, g=g, d=d, nj=nj, kcat=kcat, vcat=vcat, q_ref=q_ref, h=h):
                r = u // nj
                j = u % nj
                row = pl.multiple_of(j * BLOCK, BLOCK)
                q = q_ref[h, r, pl.ds(row, BLOCK), :]
                kk = kcat[r, pl.ds(row, 2 * BLOCK), :]
                vv = vcat[r, pl.ds(row, 2 * BLOCK), :]
                s = lax.dot_general(q, kk, (((1,), (1,)), ((), ())), preferred_element_type=F32)
                s = s + bsc[jnp.where(j == 0, 1, 0), g]
                m = jnp.max(s, axis=1, keepdims=True)
                p = jnp.exp(s - m)
                l = jnp.sum(p, axis=1, keepdims=True)
                acc = jnp.dot(p.astype(BF16), vv, preferred_element_type=F32)
                start = j * (BLOCK * d) + r
                if d == 1:
                    rows = pl.ds(pl.multiple_of(start, BLOCK), BLOCK)
                else:
                    rows = pl.ds(start, BLOCK, stride=d)
                oscr[g, rows, :] = acc
                mscr[g, rows, :] = jnp.broadcast_to(m, (BLOCK, HEAD_DIM))
                lscr[g, rows, :] = jnp.broadcast_to(l, (BLOCK, HEAD_DIM))
                return carry

            lax.fori_loop(0, d * nj, unit, 0, unroll=ATTN_UNROLL)

        chunk = 2 * BLOCK

        def merge(c, carry, h=h):
            rows = pl.ds(pl.multiple_of(c * chunk, chunk), chunk)
            ms = [mscr[g, rows, :] for g in range(n_g)]
            mx = functools.reduce(jnp.maximum, ms)
            es = [jnp.exp(m - mx) for m in ms]
            num = sum(e * oscr[g, rows, :] for g, e in enumerate(es))
            den = sum(e * lscr[g, rows, :] for g, e in enumerate(es))
            o_ref[rows, h * HEAD_DIM:(h + 1) * HEAD_DIM] = (num * (1.0 / den)).astype(BF16)
            return carry

        lax.fori_loop(0, ATTN_TILE // chunk, merge, 0)


def _attention(qs, ks, vs, bias, batch, seq, n_heads):
    n_g = len(DILATIONS)
    hb = ATTN_HEADS if n_heads % ATTN_HEADS == 0 else 1
    in_specs, args = [], []
    for g, d in enumerate(DILATIONS):
        n = ATTN_TILE // d
        in_specs.append(pl.BlockSpec((None, hb, d, n, HEAD_DIM), lambda b, h, t: (b, h, 0, t, 0)))
        args.append(qs[g])
    for src in (ks, vs):
        for g, d in enumerate(DILATIONS):
            n = ATTN_TILE // d
            nb = n // BLOCK
            in_specs.append(pl.BlockSpec((None, hb, d, n, HEAD_DIM), lambda b, h, t: (b, h, 0, t, 0)))
            in_specs.append(pl.BlockSpec((None, hb, d, BLOCK, HEAD_DIM),
                                         lambda b, h, t, nb=nb: (b, h, 0, jnp.maximum(t * nb - 1, 0), 0)))
            args += [src[g], src[g]]
    in_specs.append(pl.BlockSpec((n_g, hb, BLOCK, 2 * BLOCK), lambda b, h, t: (0, h, 0, 0)))
    args.append(bias)
    scratch = [pltpu.VMEM((d, BLOCK + ATTN_TILE // d, HEAD_DIM), BF16) for d in DILATIONS] * 2
    scratch += [pltpu.VMEM((2, n_g, BLOCK, 2 * BLOCK), F32)]
    scratch += [pltpu.VMEM((n_g, ATTN_TILE, HEAD_DIM), F32)] * 3
    return pl.pallas_call(
        functools.partial(_attn_body, hb=hb),
        out_shape=jax.ShapeDtypeStruct((batch, seq, n_heads * HEAD_DIM), BF16),
        grid=(batch, n_heads // hb, seq // ATTN_TILE),
        in_specs=in_specs,
        out_specs=pl.BlockSpec((None, ATTN_TILE, hb * HEAD_DIM), lambda b, h, t: (b, t, h)),
        scratch_shapes=scratch,
        compiler_params=_params(3),
        name="dilated_attn",
    )(*args)


def _gather_body(nrows_ref, idx_ref, idxn_ref, src_ref, o_ref, buf, sem, *, rb, nseg):
    i = pl.program_id(0)
    n = pl.num_programs(0)
    holds_rows = lambda blk: blk * rb < nrows_ref[0]

    def start_block(ids_ref, slot):
        def issue(r, c):
            pltpu.make_async_copy(src_ref.at[pl.ds(ids_ref[0, r] * nseg, nseg)],
                                  buf.at[slot, pl.ds(r * nseg, nseg)], sem.at[slot]).start()
            return c
        lax.fori_loop(0, rb, issue, 0, unroll=8)

    @pl.when((i == 0) & holds_rows(0))
    def _():
        start_block(idx_ref, 0)

    @pl.when((i + 1 < n) & holds_rows(i + 1))
    def _():
        start_block(idxn_ref, (i + 1) % 2)

    half = nseg * V7X_LANES
    for slot in range(2):
        @pl.when(holds_rows(i) & (i % 2 == slot))
        def _(slot=slot):
            pltpu.make_async_copy(src_ref.at[pl.ds(0, rb * nseg)], buf.at[slot], sem.at[slot]).wait()
            for a in range(nseg):
                u = buf[slot, pl.ds(a, rb, stride=nseg), :]
                o_ref[:, a * V7X_LANES:(a + 1) * V7X_LANES] = pltpu.bitcast(u << 16, F32).astype(BF16)
                o_ref[:, half + a * V7X_LANES:half + (a + 1) * V7X_LANES] = (
                    pltpu.bitcast(u & jnp.uint32(0xFFFF0000), F32).astype(BF16))

    @pl.when(jnp.logical_not(holds_rows(i)))
    def _():
        o_ref[...] = jnp.zeros_like(o_ref)


def _gather_rows(n_rows_valid, src_idx, hp, d, rb):
    n_rows = src_idx.shape[0]
    nseg = d // 2 // V7X_LANES
    nblk = n_rows // rb
    idx3 = src_idx.reshape(nblk, 1, rb)
    gs = pltpu.PrefetchScalarGridSpec(
        num_scalar_prefetch=1,
        grid=(nblk,),
        in_specs=[pl.BlockSpec((None, 1, rb), lambda i, nr: (i, 0, 0), memory_space=pltpu.SMEM),
                  pl.BlockSpec((None, 1, rb), lambda i, nr: (jnp.minimum(i + 1, nblk - 1), 0, 0),
                               memory_space=pltpu.SMEM),
                  pl.BlockSpec(memory_space=pl.ANY)],
        out_specs=pl.BlockSpec((rb, d), lambda i, nr: (i, 0)),
        scratch_shapes=[pltpu.VMEM((2, rb * nseg, V7X_LANES), jnp.uint32), pltpu.SemaphoreType.DMA((2,))],
    )
    return pl.pallas_call(
        functools.partial(_gather_body, rb=rb, nseg=nseg),
        out_shape=jax.ShapeDtypeStruct((n_rows, d), BF16),
        grid_spec=gs,
        compiler_params=_params(1),
        name="moe_gather",
    )(n_rows_valid, idx3, idx3, hp)


def _combine_body(pos_ref, posn_ref, y_ref, x_ref, route_ref, mod_ref, o_ref, buf, sem, *, tc, gate_row):
    i = pl.program_id(0)
    n = pl.num_programs(0)

    def start_block(p_ref, slot):
        def issue(r, c):
            pltpu.make_async_copy(y_ref.at[pl.ds(p_ref[0, r], 1)], buf.at[slot, 0, pl.ds(r, 1)], sem.at[slot]).start()
            pltpu.make_async_copy(y_ref.at[pl.ds(p_ref[0, tc + r], 1)], buf.at[slot, 1, pl.ds(r, 1)],
                                  sem.at[slot]).start()
            return c
        lax.fori_loop(0, tc, issue, 0, unroll=8)

    @pl.when(i == 0)
    def _():
        start_block(pos_ref, 0)

    @pl.when(i + 1 < n)
    def _():
        start_block(posn_ref, (i + 1) % 2)

    g1 = route_ref[:, 2:3]
    g2 = route_ref[:, 3:4]
    for slot in range(2):
        @pl.when(i % 2 == slot)
        def _(slot=slot):
            pltpu.make_async_copy(y_ref.at[pl.ds(0, tc)], buf.at[slot, 0], sem.at[slot]).wait()
            pltpu.make_async_copy(y_ref.at[pl.ds(0, tc)], buf.at[slot, 1], sem.at[slot]).wait()
            y = g1 * buf[slot, 0] + g2 * buf[slot, 1]
            o_ref[...] = x_ref[...] + mod_ref[gate_row:gate_row + 1, :] * y


def _combine(pos, y, x, route, mod, gate_row):
    m, d = x.shape
    b = mod.shape[0]
    s = m // b
    tc = pos.shape[2] // 2
    nblk = m // tc
    return pl.pallas_call(
        functools.partial(_combine_body, tc=tc, gate_row=gate_row),
        out_shape=jax.ShapeDtypeStruct((m, d), F32),
        grid=(nblk,),
        in_specs=[pl.BlockSpec((None, 1, 2 * tc), lambda i: (i, 0, 0), memory_space=pltpu.SMEM),
                  pl.BlockSpec((None, 1, 2 * tc), lambda i: (jnp.minimum(i + 1, nblk - 1), 0, 0),
                               memory_space=pltpu.SMEM),
                  pl.BlockSpec(memory_space=pl.ANY),
                  pl.BlockSpec((tc, d), lambda i: (i, 0)),
                  pl.BlockSpec((tc, V7X_LANES), lambda i: (i, 0)),
                  pl.BlockSpec((None, 8, d), lambda i: ((i * tc) // s, 0, 0))],
        out_specs=pl.BlockSpec((tc, d), lambda i: (i, 0)),
        scratch_shapes=[pltpu.VMEM((2, 2, tc, d), F32), pltpu.SemaphoreType.DMA((2,))],
        compiler_params=_params(1),
        name="moe_combine",
    )(pos, pos, y, x, route, mod)


def _rows8(*rows):
    b, d = rows[0].shape
    pad = [jnp.zeros((b, d), F32)] * (8 - len(rows))
    return jnp.stack(list(rows) + pad, axis=1)


def _par8(*rows):
    d = rows[0].shape[0]
    pad = [jnp.zeros((d,), F32)] * (8 - len(rows))
    return jnp.stack(list(rows) + pad, axis=0)


def kernel(x, c, ada_w, ada_b, norm_g, pool_w, pool_scale, kv_ada_w, kv_ada_b, kv_norm_g, w_k, w_v, k_norm_g,
           w_q, q_norm_g, w_o, rel_bias, ffn_w1, ffn_w3, ffn_w2, router_w, moe_w1, moe_w3, moe_w2):
    batch, seq, d = x.shape
    n_tok = batch * seq
    n_heads = d // HEAD_DIM
    n_exp = router_w.shape[2]
    assert d % HEAD_DIM == 0 and seq % ATTN_TILE == 0 and batch <= 8

    c8 = jnp.pad(c, ((0, 8 - batch), (0, 0)))
    mods = _ada(c8, ada_w.reshape(4, d, 3 * d), ada_b.reshape(4, 1, 3 * d))[:, :batch]
    kvm = _ada(c8, kv_ada_w[None], kv_ada_b[None, None])[0, :batch]
    sh = lambda m: (mods[m, :, :d], mods[m, :, d:2 * d], mods[m, :, 2 * d:])
    (s00, c00, g00), (s01, c01, g01), (s10, c10, g10), (s11, c11, g11) = sh(0), sh(1), sh(2), sh(3)
    kv_shift, kv_scale = kvm[:, :d], kvm[:, d:]

    x1, h2 = _mixer(x, _rows8(s00, c00, g00, s01, c01), _par8(norm_g[0, 0], norm_g[0, 1], pool_scale[0]),
                    pool_w[0].astype(BF16))
    tm_ffn = _pick(n_tok, 2048, 16)
    gact = _swiglu_up(jnp.zeros((n_tok // tm_ffn,), jnp.int32), jnp.full((1,), n_tok // tm_ffn, jnp.int32),
                      h2.reshape(n_tok, d), ffn_w1[:1], ffn_w3[:1], tm_ffn, 256, "ffn_up")
    x2 = _matmul_residual(gact, ffn_w2[0].astype(BF16), x1.reshape(n_tok, d), _rows8(g01), 0, 512, 512, "ffn_down")

    kvh, h3 = _norm2(x2.reshape(batch, seq, d), _rows8(kv_shift, kv_scale, s10, c10), _par8(kv_norm_g, norm_g[1, 0]))
    kvh, h3 = kvh.reshape(n_tok, d), h3.reshape(n_tok, d)
    ks = _proj_heads(kvh, w_k, 0, k_norm_g.reshape(1, HEAD_DIM), DILATIONS, True, 1.0, batch, "proj_k")
    vs = _proj_heads(kvh, w_v, 0, k_norm_g.reshape(1, HEAD_DIM), DILATIONS, False, 1.0, batch, "proj_v")
    qs = [_proj_heads(h3, w_q[0], g * d, q_norm_g[0].reshape(1, HEAD_DIM), (dl,), True, HEAD_DIM ** -0.5, batch,
                      f"proj_q{dl}")[0] for g, dl in enumerate(DILATIONS)]

    bias = _bias_bands(rel_bias, n_heads)
    att = _attention(qs, ks, vs, bias, batch, seq, n_heads)
    x3 = _matmul_residual(att.reshape(n_tok, d), w_o[0].astype(BF16), x2, _rows8(g10), 0, 1024, 512, "attn_out")

    rw_pad = jnp.pad(router_w[0], ((0, 0), (0, V7X_LANES - n_exp)))
    hp, route = _route(x3.reshape(batch, seq, d), _rows8(s11, c11), _par8(norm_g[1, 1]), rw_pad, n_exp)
    route = route.reshape(n_tok, V7X_LANES)

    tm = _pick(n_tok, 512, 256)
    r_pad = TOP_K * n_tok + n_exp * tm
    e_flat = jnp.concatenate([route[:, 0], route[:, 1]]).astype(jnp.int32)
    onehot = (e_flat[:, None] == jnp.arange(n_exp, dtype=jnp.int32)[None, :]).astype(jnp.int32)
    csum = jnp.cumsum(onehot, axis=0)
    rank = jnp.sum((csum - 1) * onehot, axis=1)
    padded = ((csum[-1] + tm - 1) // tm) * tm
    gend = jnp.cumsum(padded)
    dest = (gend - padded)[e_flat] + rank
    tok = jnp.arange(TOP_K * n_tok, dtype=jnp.int32) % n_tok
    src_idx = jnp.zeros((r_pad,), jnp.int32).at[dest].set(tok)
    tile_expert = jnp.minimum(
        jnp.searchsorted(gend, jnp.arange(r_pad // tm, dtype=jnp.int32) * tm, side="right"), n_exp - 1
    ).astype(jnp.int32)

    n_valid = (gend[-1:] // tm).astype(jnp.int32)

    xg = _gather_rows(gend[-1:].astype(jnp.int32), src_idx, hp, d, 256)
    hg = _swiglu_up(tile_expert, n_valid, xg, moe_w1[0], moe_w3[0], tm, 512, "moe_up")
    yg = _down_grouped(tile_expert, n_valid, hg, moe_w2[0], tm)

    tc = _pick(n_tok, 256, 8)
    pos = jnp.concatenate([dest[:n_tok].reshape(n_tok // tc, 1, tc), dest[n_tok:].reshape(n_tok // tc, 1, tc)], axis=2)
    out = _combine(pos.astype(jnp.int32), yg, x3, route, _rows8(g11), 0)
    return out.reshape(batch, seq, d)
```

```python
import functools
import math

import numpy as np
import jax
import jax.numpy as jnp
from jax import lax
from jax.experimental import pallas as pl
from jax.experimental.pallas import tpu as pltpu

F32 = jnp.float32
BF16 = jnp.bfloat16

V7X_LANES = 128
V7X_VMEM_LIMIT_BYTES = 60 * 1024 * 1024

HEAD_DIM = 128
BLOCK = 128
DILATIONS = (1, 4, 16)
N_TAPS = 128
NUM_BUCKETS = 32
MAX_DISTANCE = 2048
POOL_WINDOWS = (2, 4, 8, 16)
POOL_HALO = 16
TOP_K = 2
EPS = 1e-6
NEG_INF = -1e30
ATTN_TILE = BLOCK * DILATIONS[-1]
ATTN_UNROLL = 16
ATTN_HEADS = 2


def _params(n_grid, vmem=V7X_VMEM_LIMIT_BYTES):
    return pltpu.CompilerParams(dimension_semantics=("arbitrary",) * n_grid, vmem_limit_bytes=vmem)


def _pick(n, pref, mult):
    if n <= pref:
        return n
    t = (pref // mult) * mult
    while t >= mult:
        if n % t == 0:
            return t
        t -= mult
    return n


def _silu(v):
    return v * jax.nn.sigmoid(v)


def _norm_mod(v, g, scale, shift):
    ms = jnp.mean(v * v, axis=-1, keepdims=True)
    return (v * lax.rsqrt(ms + EPS)) * (g * (1.0 + scale)) + shift


def _ada_body(c_ref, w_ref, b_ref, o_ref):
    cond = _silu(c_ref[...]).astype(BF16)
    o_ref[...] = jnp.dot(cond, w_ref[...].astype(BF16), preferred_element_type=F32) + b_ref[...]


def _ada(c8, w, b):
    m, d, n = w.shape
    bn = _pick(n, 512, V7X_LANES)
    return pl.pallas_call(
        _ada_body,
        out_shape=jax.ShapeDtypeStruct((m, 8, n), F32),
        grid=(m, n // bn),
        in_specs=[
            pl.BlockSpec((8, d), lambda i, j: (0, 0)),
            pl.BlockSpec((None, d, bn), lambda i, j: (i, 0, j)),
            pl.BlockSpec((None, 1, bn), lambda i, j: (i, 0, j)),
        ],
        out_specs=pl.BlockSpec((None, 8, bn), lambda i, j: (i, 0, j)),
        compiler_params=_params(2),
        name="ada",
    )(c8, w, b)


def _mixer_body(x_ref, halo_ref, mod_ref, par_ref, pw_ref, x1_ref, h2_ref, *, ts, group):
    i = pl.program_id(1)
    x = x_ref[...]
    g_a, g_f, pscale = par_ref[0:1, :], par_ref[1:2, :], par_ref[2:3, :]
    shift_a, scale_a, gate_a = mod_ref[0:1, :], mod_ref[1:2, :], mod_ref[2:3, :]
    shift_f, scale_f = mod_ref[3:4, :], mod_ref[4:5, :]

    h = _norm_mod(x, g_a, scale_a, shift_a)
    h_halo = jnp.where(i > 0, _norm_mod(halo_ref[...], g_a, scale_a, shift_a), 0.0)
    hh = jnp.concatenate([h_halo, h], axis=0)

    lt = min(V7X_LANES, group)
    pos = i * ts + lax.broadcasted_iota(jnp.int32, (ts, lt), 0)
    ys = []
    for g, k in enumerate(POOL_WINDOWS):
        sl = slice(g * group, (g + 1) * group)
        a = hh[:, sl]
        span = 1
        while span < k:
            a = a[:-span] + a[span:]
            span *= 2
        start = POOL_HALO + 1 - k
        win = a[start:start + ts]
        inv_cnt = 1.0 / jnp.minimum(pos + 1, k).astype(F32)
        hg = h[:, sl]
        p = jnp.concatenate([win[:, c:c + lt] * inv_cnt - hg[:, c:c + lt] for c in range(0, group, lt)], axis=1)
        ys.append(jnp.dot(p.astype(BF16), pw_ref[g], preferred_element_type=F32))
    y = jnp.concatenate(ys, axis=1) * pscale
    x1 = x + gate_a * y
    x1_ref[...] = x1
    h2_ref[...] = _norm_mod(x1, g_f, scale_f, shift_f).astype(BF16)


def _mixer(x, mod, par, pool_w):
    b, s, d = x.shape
    group = d // len(POOL_WINDOWS)
    ts = _pick(s, 256, POOL_HALO)
    hpb = ts // POOL_HALO
    return pl.pallas_call(
        functools.partial(_mixer_body, ts=ts, group=group),
        out_shape=(jax.ShapeDtypeStruct((b, s, d), F32), jax.ShapeDtypeStruct((b, s, d), BF16)),
        grid=(b, s // ts),
        in_specs=[
            pl.BlockSpec((None, ts, d), lambda bi, i: (bi, i, 0)),
            pl.BlockSpec((None, POOL_HALO, d), lambda bi, i: (bi, jnp.maximum(i * hpb - 1, 0), 0)),
            pl.BlockSpec((None, 8, d), lambda bi, i: (bi, 0, 0)),
            pl.BlockSpec((8, d), lambda bi, i: (0, 0)),
            pl.BlockSpec((len(POOL_WINDOWS), group, group), lambda bi, i: (0, 0, 0),
                         pipeline_mode=pl.Buffered(1)),
        ],
        out_specs=(pl.BlockSpec((None, ts, d), lambda bi, i: (bi, i, 0)),
                   pl.BlockSpec((None, ts, d), lambda bi, i: (bi, i, 0))),
        compiler_params=_params(2),
        name="mixer0",
    )(x, x, mod, par, pool_w)


def _norm2_body(x_ref, mod_ref, par_ref, a_ref, b_ref):
    x = x_ref[...]
    a_ref[...] = _norm_mod(x, par_ref[0:1, :], mod_ref[1:2, :], mod_ref[0:1, :]).astype(BF16)
    b_ref[...] = _norm_mod(x, par_ref[1:2, :], mod_ref[3:4, :], mod_ref[2:3, :]).astype(BF16)


def _norm2(x, mod, par):
    b, s, d = x.shape
    ts = _pick(s, 512, 16)
    row = pl.BlockSpec((None, ts, d), lambda bi, i: (bi, i, 0))
    return pl.pallas_call(
        _norm2_body,
        out_shape=(jax.ShapeDtypeStruct((b, s, d), BF16), jax.ShapeDtypeStruct((b, s, d), BF16)),
        grid=(b, s // ts),
        in_specs=[row, pl.BlockSpec((None, 8, d), lambda bi, i: (bi, 0, 0)),
                  pl.BlockSpec((8, d), lambda bi, i: (0, 0))],
        out_specs=(row, row),
        compiler_params=_params(2),
        name="norm_kv_q",
    )(x, mod, par)


def _route_body(x_ref, mod_ref, par_ref, rwh_ref, rwl_ref, hp_ref, route_ref, *, n_exp):
    x = x_ref[...]
    h = _norm_mod(x, par_ref[0:1, :], mod_ref[1:2, :], mod_ref[0:1, :])
    d = h.shape[1]
    bits = pltpu.bitcast(h.astype(BF16).astype(F32), jnp.uint32)
    packed = (bits[:, : d // 2] >> 16) | (bits[:, d // 2:] & jnp.uint32(0xFFFF0000))
    nseg = d // 2 // V7X_LANES
    ts = h.shape[0]
    for a in range(nseg):
        hp_ref[pl.ds(a, ts, stride=nseg), :] = packed[:, a * V7X_LANES:(a + 1) * V7X_LANES]
    h_hi = h.astype(BF16)
    h_lo = (h - h_hi.astype(F32)).astype(BF16)
    logits = (jnp.dot(h_hi, rwh_ref[...], preferred_element_type=F32)
              + jnp.dot(h_hi, rwl_ref[...], preferred_element_type=F32)
              + jnp.dot(h_lo, rwh_ref[...], preferred_element_type=F32))
    lane = lax.broadcasted_iota(jnp.int32, logits.shape, 1)
    lane_f = lane.astype(F32)
    lg = jnp.where(lane < n_exp, logits, -jnp.inf)
    v1 = jnp.max(lg, axis=1, keepdims=True)
    i1 = jnp.min(jnp.where(lg == v1, lane_f, float(V7X_LANES)), axis=1, keepdims=True)
    lg2 = jnp.where(lane_f == i1, -jnp.inf, lg)
    v2 = jnp.max(lg2, axis=1, keepdims=True)
    i2 = jnp.min(jnp.where(lg2 == v2, lane_f, float(V7X_LANES)), axis=1, keepdims=True)
    e2 = jnp.exp(v2 - v1)
    g1 = 1.0 / (1.0 + e2)
    g2 = e2 / (1.0 + e2)
    out = jnp.where(lane == 0, i1, 0.0)
    out = jnp.where(lane == 1, i2, out)
    out = jnp.where(lane == 2, g1, out)
    out = jnp.where(lane == 3, g2, out)
    route_ref[...] = out


def _route(x, mod, par, rw_pad, n_exp):
    b, s, d = x.shape
    rw_hi = rw_pad.astype(BF16)
    rw_hi_residual = (rw_pad - rw_hi.astype(F32)).astype(BF16)
    ts = _pick(s, 512, 16)
    nseg = d // 2 // V7X_LANES
    tpb = s // ts
    return pl.pallas_call(
        functools.partial(_route_body, n_exp=n_exp),
        out_shape=(jax.ShapeDtypeStruct((b * s * nseg, V7X_LANES), jnp.uint32),
                   jax.ShapeDtypeStruct((b, s, V7X_LANES), F32)),
        grid=(b, tpb),
        in_specs=[pl.BlockSpec((None, ts, d), lambda bi, i: (bi, i, 0)),
                  pl.BlockSpec((None, 8, d), lambda bi, i: (bi, 0, 0)),
                  pl.BlockSpec((8, d), lambda bi, i: (0, 0)),
                  pl.BlockSpec((d, V7X_LANES), lambda bi, i: (0, 0)),
                  pl.BlockSpec((d, V7X_LANES), lambda bi, i: (0, 0))],
        out_specs=(pl.BlockSpec((ts * nseg, V7X_LANES), lambda bi, i: (bi * tpb + i, 0)),
                   pl.BlockSpec((None, ts, V7X_LANES), lambda bi, i: (bi, i, 0))),
        compiler_params=_params(2),
        name="norm_route",
    )(x, mod, par, rw_hi, rw_hi_residual)


def _weights_changed(te_ref, i):
    return (i == 0) | (te_ref[i] != te_ref[jnp.maximum(i - 1, 0)])


def _up_body(te_ref, nv_ref, a_ref, w1_ref, w3_ref, o_ref, w1b, w3b):
    i = pl.program_id(1)

    @pl.when(_weights_changed(te_ref, i))
    def _():
        w1b[...] = w1_ref[...].astype(BF16)
        w3b[...] = w3_ref[...].astype(BF16)

    @pl.when(i < nv_ref[0])
    def _():
        a = a_ref[...]
        h1 = jnp.dot(a, w1b[...], preferred_element_type=F32)
        h3 = jnp.dot(a, w3b[...], preferred_element_type=F32)
        o_ref[...] = (_silu(h1) * h3).astype(BF16)

    @pl.when(i >= nv_ref[0])
    def _():
        o_ref[...] = jnp.zeros_like(o_ref)


def _swiglu_up(tile_expert, n_valid, a, w1, w3, tm, bn_pref, name):
    m, k = a.shape
    f = w1.shape[2]
    bn = _pick(f, bn_pref, V7X_LANES)
    gs = pltpu.PrefetchScalarGridSpec(
        num_scalar_prefetch=2,
        grid=(f // bn, m // tm),
        in_specs=[pl.BlockSpec((tm, k), lambda j, i, te, nv: (jnp.minimum(i, nv[0] - 1), 0)),
                  pl.BlockSpec((None, k, bn), lambda j, i, te, nv: (te[i], 0, j)),
                  pl.BlockSpec((None, k, bn), lambda j, i, te, nv: (te[i], 0, j))],
        out_specs=pl.BlockSpec((tm, bn), lambda j, i, te, nv: (i, j)),
        scratch_shapes=[pltpu.VMEM((k, bn), BF16), pltpu.VMEM((k, bn), BF16)],
    )
    return pl.pallas_call(
        _up_body,
        out_shape=jax.ShapeDtypeStruct((m, f), BF16),
        grid_spec=gs,
        compiler_params=_params(2),
        name=name,
    )(tile_expert, n_valid, a, w1, w3)


def _down_res_body(a_ref, w_ref, x_ref, mod_ref, o_ref, *, gate_row):
    y = jnp.dot(a_ref[...], w_ref[...].astype(BF16), preferred_element_type=F32)
    o_ref[...] = x_ref[...] + mod_ref[gate_row:gate_row + 1, :] * y


def _matmul_residual(a, w, xres, mod, gate_row, tm_pref, bn_pref, name):
    m, k = a.shape
    n = w.shape[1]
    b = mod.shape[0]
    s = m // b
    tm = _pick(s, tm_pref, 16)
    bn = _pick(n, bn_pref, V7X_LANES)
    return pl.pallas_call(
        functools.partial(_down_res_body, gate_row=gate_row),
        out_shape=jax.ShapeDtypeStruct((m, n), F32),
        grid=(m // tm, n // bn),
        in_specs=[pl.BlockSpec((tm, k), lambda i, j: (i, 0)),
                  pl.BlockSpec((k, bn), lambda i, j: (0, j)),
                  pl.BlockSpec((tm, bn), lambda i, j: (i, j)),
                  pl.BlockSpec((None, 8, bn), lambda i, j: ((i * tm) // s, 0, j))],
        out_specs=pl.BlockSpec((tm, bn), lambda i, j: (i, j)),
        compiler_params=_params(2),
        name=name,
    )(a, w, xres, mod)


def _down_grouped_body(te_ref, nv_ref, a_ref, w_ref, o_ref, wb):
    i = pl.program_id(1)

    @pl.when(_weights_changed(te_ref, i))
    def _():
        wb[...] = w_ref[...].astype(BF16)

    @pl.when(i < nv_ref[0])
    def _():
        o_ref[...] = jnp.dot(a_ref[...], wb[...], preferred_element_type=F32)

    @pl.when(i >= nv_ref[0])
    def _():
        o_ref[...] = jnp.zeros_like(o_ref)


def _down_grouped(tile_expert, n_valid, a, w, tm):
    m, k = a.shape
    n = w.shape[2]
    bn = _pick(n, 1024, V7X_LANES)
    gs = pltpu.PrefetchScalarGridSpec(
        num_scalar_prefetch=2,
        grid=(n // bn, m // tm),
        in_specs=[pl.BlockSpec((tm, k), lambda j, i, te, nv: (jnp.minimum(i, nv[0] - 1), 0)),
                  pl.BlockSpec((None, k, bn), lambda j, i, te, nv: (te[i], 0, j))],
        out_specs=pl.BlockSpec((tm, bn), lambda j, i, te, nv: (i, j)),
        scratch_shapes=[pltpu.VMEM((k, bn), BF16)],
    )
    return pl.pallas_call(
        _down_grouped_body,
        out_shape=jax.ShapeDtypeStruct((m, n), F32),
        grid_spec=gs,
        compiler_params=_params(2),
        name="moe_down",
    )(tile_expert, n_valid, a, w)


def _proj_body(a_ref, w_ref, g_ref, *rest, dils, normed, out_scale, tm, n_row_tiles):
    o_refs, (ybuf, wb) = rest[:len(dils)], rest[len(dils):]
    s = pl.program_id(0)
    n_slab = ybuf.shape[0]

    @pl.when(s == 0)
    def _():
        ybuf[...] = jnp.zeros_like(ybuf)

    @pl.when(s % n_row_tiles == 0)
    def _():
        wb[...] = w_ref[...].astype(BF16)

    gvec = g_ref[...] * out_scale

    def head_norm(ys):
        ms = jnp.mean(ys * ys, axis=-1, keepdims=True)
        return (ys * lax.rsqrt(ms + EPS)) * gvec

    for o_ref, d in zip(o_refs, dils):
        for sl in range(n_slab):
            for r in range(d):
                ys = ybuf[sl] if d == 1 else ybuf[sl, pl.ds(r, tm // d, stride=d), :]
                if normed:
                    ys = head_norm(ys)
                o_ref[sl, r] = ys.astype(BF16)

    y = jnp.dot(a_ref[...], wb[...], preferred_element_type=F32)
    for sl in range(n_slab):
        ybuf[sl] = y[:, sl * HEAD_DIM:(sl + 1) * HEAD_DIM]


def _proj_heads(a, w, col0, gvec, dils, normed, out_scale, batch, name):
    m, k = a.shape
    s = m // batch
    d_model = k
    n_heads = d_model // HEAD_DIM
    tm = _pick(s, 1024, 16 * max(dils))
    bn = _pick(d_model, 512, HEAD_DIM)
    hpb = bn // HEAD_DIM
    tpb = s // tm
    ni = m // tm
    n_steps = ni * (d_model // bn)
    cb0 = col0 // bn
    cur = lambda t: jnp.minimum(t, n_steps - 1)
    prev = lambda t: jnp.maximum(t - 1, 0)
    outs = tuple(jax.ShapeDtypeStruct((batch, n_heads, d, s // d, HEAD_DIM), BF16) for d in dils)
    out_specs = tuple(
        pl.BlockSpec((None, hpb, d, tm // d, HEAD_DIM),
                     lambda t: ((prev(t) % ni) // tpb, prev(t) // ni, 0, (prev(t) % ni) % tpb, 0)) for d in dils)
    return pl.pallas_call(
        functools.partial(_proj_body, dils=dils, normed=normed, out_scale=out_scale, tm=tm, n_row_tiles=ni),
        out_shape=outs,
        grid=(n_steps + 1,),
        in_specs=[pl.BlockSpec((tm, k), lambda t: (cur(t) % ni, 0)),
                  pl.BlockSpec((k, bn), lambda t: (0, cb0 + cur(t) // ni)),
                  pl.BlockSpec((1, HEAD_DIM), lambda t: (0, 0))],
        out_specs=out_specs,
        scratch_shapes=[pltpu.VMEM((hpb, tm, HEAD_DIM), F32), pltpu.VMEM((k, bn), BF16)],
        compiler_params=_params(1),
        name=name,
    )(a, w, gvec)


def _band_buckets():
    i = np.arange(BLOCK)[:, None]
    kk = np.arange(2 * BLOCK)[None, :]
    rel = i + BLOCK - kk
    valid = (rel >= 0) & (rel <= N_TAPS)
    max_exact = NUM_BUCKETS // 2
    out = []
    for d in DILATIONS:
        n = np.maximum(rel, 0) * d
        nf = np.maximum(n, 1).astype(np.float32)
        large = max_exact + (np.log(nf / np.float32(max_exact)) / np.float32(math.log(MAX_DISTANCE / max_exact))
                             * np.float32(NUM_BUCKETS - max_exact)).astype(np.int32)
        bucket = np.where(n < max_exact, n, np.minimum(large, NUM_BUCKETS - 1))
        out.append(np.where(valid, bucket, -1))
    return np.stack(out).astype(np.int32)


def _bias_body(rb_ref, bk_ref, o_ref, *, n_heads, heads_per_step):
    g = pl.program_id(0)
    h0 = pl.program_id(1) * heads_per_step
    bucket = bk_ref[...]
    for hh in range(heads_per_step):
        acc = jnp.full(bucket.shape, NEG_INF, F32)
        for b in range(NUM_BUCKETS):
            acc = jnp.where(bucket == b, rb_ref[(b * len(DILATIONS) + g) * n_heads + h0 + hh], acc)
        o_ref[hh] = acc


def _bias_bands(rel_bias, n_heads):
    buckets = jnp.asarray(_band_buckets())
    n_g = len(DILATIONS)
    hps = _pick(n_heads, 8, 1)
    return pl.pallas_call(
        functools.partial(_bias_body, n_heads=n_heads, heads_per_step=hps),
        out_shape=jax.ShapeDtypeStruct((n_g, n_heads, BLOCK, 2 * BLOCK), F32),
        grid=(n_g, n_heads // hps),
        in_specs=[pl.BlockSpec(memory_space=pltpu.SMEM),
                  pl.BlockSpec((None, BLOCK, 2 * BLOCK), lambda g, h: (g, 0, 0))],
        out_specs=pl.BlockSpec((None, hps, BLOCK, 2 * BLOCK), lambda g, h: (g, h, 0, 0)),
        compiler_params=_params(2),
        name="bias_bands",
    )(rel_bias.reshape(-1), buckets)


def _attn_body(*refs, hb):
    n_g = len(DILATIONS)
    q_refs = refs[0:n_g]
    k_refs = refs[n_g:3 * n_g:2]
    kh_refs = refs[n_g + 1:3 * n_g:2]
    v_refs = refs[3 * n_g:5 * n_g:2]
    vh_refs = refs[3 * n_g + 1:5 * n_g:2]
    bias_ref = refs[5 * n_g]
    o_ref = refs[5 * n_g + 1]
    scr = refs[5 * n_g + 2:]
    kcats, vcats = scr[0:n_g], scr[n_g:2 * n_g]
    bsc, oscr, mscr, lscr = scr[2 * n_g:2 * n_g + 4]

    ti = pl.program_id(2)
    col = lax.broadcasted_iota(jnp.int32, (BLOCK, 2 * BLOCK), 1)
    pen = jnp.where((col < BLOCK) & (ti == 0), NEG_INF, 0.0)

    for h in range(hb):
        for g in range(n_g):
            band = bias_ref[g, h]
            bsc[0, g] = band
            bsc[1, g] = band + pen

        for g, d in enumerate(DILATIONS):
            n = ATTN_TILE // d
            nj = n // BLOCK
            kcat, vcat = kcats[g], vcats[g]
            kcat[:, 0:BLOCK, :] = kh_refs[g][h]
            kcat[:, BLOCK:, :] = k_refs[g][h]
            vcat[:, 0:BLOCK, :] = vh_refs[g][h]
            vcat[:, BLOCK:, :] = v_refs[g][h]
            q_ref = q_refs[g]

            def unit(u, carry, g=g, d=d, nj=nj, kcat=kcat, vcat=vcat, q_ref=q_ref, h=h):
                r = u // nj
                j = u % nj
                row = pl.multiple_of(j * BLOCK, BLOCK)
                q = q_ref[h, r, pl.ds(row, BLOCK), :]
                kk = kcat[r, pl.ds(row, 2 * BLOCK), :]
                vv = vcat[r, pl.ds(row, 2 * BLOCK), :]
                s = lax.dot_general(q, kk, (((1,), (1,)), ((), ())), preferred_element_type=F32)
                s = s + bsc[jnp.where(j == 0, 1, 0), g]
                m = jnp.max(s, axis=1, keepdims=True)
                p = jnp.exp(s - m)
                l = jnp.sum(p, axis=1, keepdims=True)
                acc = jnp.dot(p.astype(BF16), vv, preferred_element_type=F32)
                start = j * (BLOCK * d) + r
                if d == 1:
                    rows = pl.ds(pl.multiple_of(start, BLOCK), BLOCK)
                else:
                    rows = pl.ds(start, BLOCK, stride=d)
                oscr[g, rows, :] = acc
                mscr[g, rows, :] = jnp.broadcast_to(m, (BLOCK, HEAD_DIM))
                lscr[g, rows, :] = jnp.broadcast_to(l, (BLOCK, HEAD_DIM))
                return carry

            lax.fori_loop(0, d * nj, unit, 0, unroll=ATTN_UNROLL)

        chunk = 2 * BLOCK

        def merge(c, carry, h=h):
            rows = pl.ds(pl.multiple_of(c * chunk, chunk), chunk)
            ms = [mscr[g, rows, :] for g in range(n_g)]
            mx = functools.reduce(jnp.maximum, ms)
            es = [jnp.exp(m - mx) for m in ms]
            num = sum(e * oscr[g, rows, :] for g, e in enumerate(es))
            den = sum(e * lscr[g, rows, :] for g, e in enumerate(es))
            o_ref[rows, h * HEAD_DIM:(h + 1) * HEAD_DIM] = (num * (1.0 / den)).astype(BF16)
            return carry

        lax.fori_loop(0, ATTN_TILE // chunk, merge, 0)


def _attention(qs, ks, vs, bias, batch, seq, n_heads):
    n_g = len(DILATIONS)
    hb = ATTN_HEADS if n_heads % ATTN_HEADS == 0 else 1
    in_specs, args = [], []
    for g, d in enumerate(DILATIONS):
        n = ATTN_TILE // d
        in_specs.append(pl.BlockSpec((None, hb, d, n, HEAD_DIM), lambda b, h, t: (b, h, 0, t, 0)))
        args.append(qs[g])
    for src in (ks, vs):
        for g, d in enumerate(DILATIONS):
            n = ATTN_TILE // d
            nb = n // BLOCK
            in_specs.append(pl.BlockSpec((None, hb, d, n, HEAD_DIM), lambda b, h, t: (b, h, 0, t, 0)))
            in_specs.append(pl.BlockSpec((None, hb, d, BLOCK, HEAD_DIM),
                                         lambda b, h, t, nb=nb: (b, h, 0, jnp.maximum(t * nb - 1, 0), 0)))
            args += [src[g], src[g]]
    in_specs.append(pl.BlockSpec((n_g, hb, BLOCK, 2 * BLOCK), lambda b, h, t: (0, h, 0, 0)))
    args.append(bias)
    scratch = [pltpu.VMEM((d, BLOCK + ATTN_TILE // d, HEAD_DIM), BF16) for d in DILATIONS] * 2
    scratch += [pltpu.VMEM((2, n_g, BLOCK, 2 * BLOCK), F32)]
    scratch += [pltpu.VMEM((n_g, ATTN_TILE, HEAD_DIM), F32)] * 3
    return pl.pallas_call(
        functools.partial(_attn_body, hb=hb),
        out_shape=jax.ShapeDtypeStruct((batch, seq, n_heads * HEAD_DIM), BF16),
        grid=(batch, n_heads // hb, seq // ATTN_TILE),
        in_specs=in_specs,
        out_specs=pl.BlockSpec((None, ATTN_TILE, hb * HEAD_DIM), lambda b, h, t: (b, t, h)),
        scratch_shapes=scratch,
        compiler_params=_params(3),
        name="dilated_attn",
    )(*args)


def _gather_body(nrows_ref, idx_ref, idxn_ref, src_ref, o_ref, buf, sem, *, rb, nseg):
    i = pl.program_id(0)
    n = pl.num_programs(0)
    holds_rows = lambda blk: blk * rb < nrows_ref[0]

    def start_block(ids_ref, slot):
        def issue(r, c):
            pltpu.make_async_copy(src_ref.at[pl.ds(ids_ref[0, r] * nseg, nseg)],
                                  buf.at[slot, pl.ds(r * nseg, nseg)], sem.at[slot]).start()
            return c
        lax.fori_loop(0, rb, issue, 0, unroll=8)

    @pl.when((i == 0) & holds_rows(0))
    def _():
        start_block(idx_ref, 0)

    @pl.when((i + 1 < n) & holds_rows(i + 1))
    def _():
        start_block(idxn_ref, (i + 1) % 2)

    half = nseg * V7X_LANES
    for slot in range(2):
        @pl.when(holds_rows(i) & (i % 2 == slot))
        def _(slot=slot):
            pltpu.make_async_copy(src_ref.at[pl.ds(0, rb * nseg)], buf.at[slot], sem.at[slot]).wait()
            for a in range(nseg):
                u = buf[slot, pl.ds(a, rb, stride=nseg), :]
                o_ref[:, a * V7X_LANES:(a + 1) * V7X_LANES] = pltpu.bitcast(u << 16, F32).astype(BF16)
                o_ref[:, half + a * V7X_LANES:half + (a + 1) * V7X_LANES] = (
                    pltpu.bitcast(u & jnp.uint32(0xFFFF0000), F32).astype(BF16))

    @pl.when(jnp.logical_not(holds_rows(i)))
    def _():
        o_ref[...] = jnp.zeros_like(o_ref)


def _gather_rows(n_rows_valid, src_idx, hp, d, rb):
    n_rows = src_idx.shape[0]
    nseg = d // 2 // V7X_LANES
    nblk = n_rows // rb
    idx3 = src_idx.reshape(nblk, 1, rb)
    gs = pltpu.PrefetchScalarGridSpec(
        num_scalar_prefetch=1,
        grid=(nblk,),
        in_specs=[pl.BlockSpec((None, 1, rb), lambda i, nr: (i, 0, 0), memory_space=pltpu.SMEM),
                  pl.BlockSpec((None, 1, rb), lambda i, nr: (jnp.minimum(i + 1, nblk - 1), 0, 0),
                               memory_space=pltpu.SMEM),
                  pl.BlockSpec(memory_space=pl.ANY)],
        out_specs=pl.BlockSpec((rb, d), lambda i, nr: (i, 0)),
        scratch_shapes=[pltpu.VMEM((2, rb * nseg, V7X_LANES), jnp.uint32), pltpu.SemaphoreType.DMA((2,))],
    )
    return pl.pallas_call(
        functools.partial(_gather_body, rb=rb, nseg=nseg),
        out_shape=jax.ShapeDtypeStruct((n_rows, d), BF16),
        grid_spec=gs,
        compiler_params=_params(1),
        name="moe_gather",
    )(n_rows_valid, idx3, idx3, hp)


def _combine_body(pos_ref, posn_ref, y_ref, x_ref, route_ref, mod_ref, o_ref, buf, sem, *, tc, gate_row):
    i = pl.program_id(0)
    n = pl.num_programs(0)

    def start_block(p_ref, slot):
        def issue(r, c):
            pltpu.make_async_copy(y_ref.at[pl.ds(p_ref[0, r], 1)], buf.at[slot, 0, pl.ds(r, 1)], sem.at[slot]).start()
            pltpu.make_async_copy(y_ref.at[pl.ds(p_ref[0, tc + r], 1)], buf.at[slot, 1, pl.ds(r, 1)],
                                  sem.at[slot]).start()
            return c
        lax.fori_loop(0, tc, issue, 0, unroll=8)

    @pl.when(i == 0)
    def _():
        start_block(pos_ref, 0)

    @pl.when(i + 1 < n)
    def _():
        start_block(posn_ref, (i + 1) % 2)

    g1 = route_ref[:, 2:3]
    g2 = route_ref[:, 3:4]
    for slot in range(2):
        @pl.when(i % 2 == slot)
        def _(slot=slot):
            pltpu.make_async_copy(y_ref.at[pl.ds(0, tc)], buf.at[slot, 0], sem.at[slot]).wait()
            pltpu.make_async_copy(y_ref.at[pl.ds(0, tc)], buf.at[slot, 1], sem.at[slot]).wait()
            y = g1 * buf[slot, 0] + g2 * buf[slot, 1]
            o_ref[...] = x_ref[...] + mod_ref[gate_row:gate_row + 1, :] * y


def _combine(pos, y, x, route, mod, gate_row):
    m, d = x.shape
    b = mod.shape[0]
    s = m // b
    tc = pos.shape[2] // 2
    nblk = m // tc
    return pl.pallas_call(
        functools.partial(_combine_body, tc=tc, gate_row=gate_row),
        out_shape=jax.ShapeDtypeStruct((m, d), F32),
        grid=(nblk,),
        in_specs=[pl.BlockSpec((None, 1, 2 * tc), lambda i: (i, 0, 0), memory_space=pltpu.SMEM),
                  pl.BlockSpec((None, 1, 2 * tc), lambda i: (jnp.minimum(i + 1, nblk - 1), 0, 0),
                               memory_space=pltpu.SMEM),
                  pl.BlockSpec(memory_space=pl.ANY),
                  pl.BlockSpec((tc, d), lambda i: (i, 0)),
                  pl.BlockSpec((tc, V7X_LANES), lambda i: (i, 0)),
                  pl.BlockSpec((None, 8, d), lambda i: ((i * tc) // s, 0, 0))],
        out_specs=pl.BlockSpec((tc, d), lambda i: (i, 0)),
        scratch_shapes=[pltpu.VMEM((2, 2, tc, d), F32), pltpu.SemaphoreType.DMA((2,))],
        compiler_params=_params(1),
        name="moe_combine",
    )(pos, pos, y, x, route, mod)


def _rows8(*rows):
    b, d = rows[0].shape
    pad = [jnp.zeros((b, d), F32)] * (8 - len(rows))
    return jnp.stack(list(rows) + pad, axis=1)


def _par8(*rows):
    d = rows[0].shape[0]
    pad = [jnp.zeros((d,), F32)] * (8 - len(rows))
    return jnp.stack(list(rows) + pad, axis=0)


def kernel(x, c, ada_w, ada_b, norm_g, pool_w, pool_scale, kv_ada_w, kv_ada_b, kv_norm_g, w_k, w_v, k_norm_g,
           w_q, q_norm_g, w_o, rel_bias, ffn_w1, ffn_w3, ffn_w2, router_w, moe_w1, moe_w3, moe_w2):
    batch, seq, d = x.shape
    n_tok = batch * seq
    n_heads = d // HEAD_DIM
    n_exp = router_w.shape[2]
    assert d % HEAD_DIM == 0 and seq % ATTN_TILE == 0 and batch <= 8

    c8 = jnp.pad(c, ((0, 8 - batch), (0, 0)))
    mods = _ada(c8, ada_w.reshape(4, d, 3 * d), ada_b.reshape(4, 1, 3 * d))[:, :batch]
    kvm = _ada(c8, kv_ada_w[None], kv_ada_b[None, None])[0, :batch]
    sh = lambda m: (mods[m, :, :d], mods[m, :, d:2 * d], mods[m, :, 2 * d:])
    (s00, c00, g00), (s01, c01, g01), (s10, c10, g10), (s11, c11, g11) = sh(0), sh(1), sh(2), sh(3)
    kv_shift, kv_scale = kvm[:, :d], kvm[:, d:]

    x1, h2 = _mixer(x, _rows8(s00, c00, g00, s01, c01), _par8(norm_g[0, 0], norm_g[0, 1], pool_scale[0]),
                    pool_w[0].astype(BF16))
    tm_ffn = _pick(n_tok, 2048, 16)
    gact = _swiglu_up(jnp.zeros((n_tok // tm_ffn,), jnp.int32), jnp.full((1,), n_tok // tm_ffn, jnp.int32),
                      h2.reshape(n_tok, d), ffn_w1[:1], ffn_w3[:1], tm_ffn, 256, "ffn_up")
    x2 = _matmul_residual(gact, ffn_w2[0].astype(BF16), x1.reshape(n_tok, d), _rows8(g01), 0, 512, 512, "ffn_down")

    kvh, h3 = _norm2(x2.reshape(batch, seq, d), _rows8(kv_shift, kv_scale, s10, c10), _par8(kv_norm_g, norm_g[1, 0]))
    kvh, h3 = kvh.reshape(n_tok, d), h3.reshape(n_tok, d)
    ks = _proj_heads(kvh, w_k, 0, k_norm_g.reshape(1, HEAD_DIM), DILATIONS, True, 1.0, batch, "proj_k")
    vs = _proj_heads(kvh, w_v, 0, k_norm_g.reshape(1, HEAD_DIM), DILATIONS, False, 1.0, batch, "proj_v")
    qs = [_proj_heads(h3, w_q[0], g * d, q_norm_g[0].reshape(1, HEAD_DIM), (dl,), True, HEAD_DIM ** -0.5, batch,
                      f"proj_q{dl}")[0] for g, dl in enumerate(DILATIONS)]

    bias = _bias_bands(rel_bias, n_heads)
    att = _attention(qs, ks, vs, bias, batch, seq, n_heads)
    x3 = _matmul_residual(att.reshape(n_tok, d), w_o[0].astype(BF16), x2, _rows8(g10), 0, 1024, 512, "attn_out")

    rw_pad = jnp.pad(router_w[0], ((0, 0), (0, V7X_LANES - n_exp)))
    hp, route = _route(x3.reshape(batch, seq, d), _rows8(s11, c11), _par8(norm_g[1, 1]), rw_pad, n_exp)
    route = route.reshape(n_tok, V7X_LANES)

    tm = _pick(n_tok, 512, 256)
    r_pad = TOP_K * n_tok + n_exp * tm
    e_flat = jnp.concatenate([route[:, 0], route[:, 1]]).astype(jnp.int32)
    onehot = (e_flat[:, None] == jnp.arange(n_exp, dtype=jnp.int32)[None, :]).astype(jnp.int32)
    csum = jnp.cumsum(onehot, axis=0)
    rank = jnp.sum((csum - 1) * onehot, axis=1)
    padded = ((csum[-1] + tm - 1) // tm) * tm
    gend = jnp.cumsum(padded)
    dest = (gend - padded)[e_flat] + rank
    tok = jnp.arange(TOP_K * n_tok, dtype=jnp.int32) % n_tok
    src_idx = jnp.zeros((r_pad,), jnp.int32).at[dest].set(tok)
    tile_expert = jnp.minimum(
        jnp.searchsorted(gend, jnp.arange(r_pad // tm, dtype=jnp.int32) * tm, side="right"), n_exp - 1
    ).astype(jnp.int32)

    n_valid = (gend[-1:] // tm).astype(jnp.int32)

    xg = _gather_rows(gend[-1:].astype(jnp.int32), src_idx, hp, d, tm)
    hg = _swiglu_up(tile_expert, n_valid, xg, moe_w1[0], moe_w3[0], tm, 512, "moe_up")
    yg = _down_grouped(tile_expert, n_valid, hg, moe_w2[0], tm)

    tc = _pick(n_tok, 256, 8)
    pos = jnp.concatenate([dest[:n_tok].reshape(n_tok // tc, 1, tc), dest[n_tok:].reshape(n_tok // tc, 1, tc)], axis=2)
    out = _combine(pos.astype(jnp.int32), yg, x3, route, _rows8(g11), 0)
    return out.reshape(batch, seq, d)
```

```python
import functools
import math

import numpy as np
import jax
import jax.numpy as jnp
from jax import lax
from jax.experimental import pallas as pl
from jax.experimental.pallas import tpu as pltpu

F32 = jnp.float32
BF16 = jnp.bfloat16

V7X_LANES = 128
V7X_VMEM_LIMIT_BYTES = 60 * 1024 * 1024

HEAD_DIM = 128
BLOCK = 128
DILATIONS = (1, 4, 16)
N_TAPS = 128
NUM_BUCKETS = 32
MAX_DISTANCE = 2048
POOL_WINDOWS = (2, 4, 8, 16)
POOL_HALO = 16
TOP_K = 2
EPS = 1e-6
NEG_INF = -1e30
ATTN_TILE = BLOCK * DILATIONS[-1]
ATTN_UNROLL = 16
ATTN_HEADS = 2


def _params(n_grid, vmem=V7X_VMEM_LIMIT_BYTES):
    return pltpu.CompilerParams(dimension_semantics=("arbitrary",) * n_grid, vmem_limit_bytes=vmem)


def _pick(n, pref, mult):
    if n <= pref:
        return n
    t = (pref // mult) * mult
    while t >= mult:
        if n % t == 0:
            return t
        t -= mult
    return n


def _silu(v):
    return v * jax.nn.sigmoid(v)


def _norm_mod(v, g, scale, shift):
    ms = jnp.mean(v * v, axis=-1, keepdims=True)
    return (v * lax.rsqrt(ms + EPS)) * (g * (1.0 + scale)) + shift


def _ada_body(c_ref, w_ref, b_ref, o_ref):
    cond = _silu(c_ref[...]).astype(BF16)
    o_ref[...] = jnp.dot(cond, w_ref[...].astype(BF16), preferred_element_type=F32) + b_ref[...]


def _ada(c8, w, b):
    m, d, n = w.shape
    bn = _pick(n, 512, V7X_LANES)
    return pl.pallas_call(
        _ada_body,
        out_shape=jax.ShapeDtypeStruct((m, 8, n), F32),
        grid=(m, n // bn),
        in_specs=[
            pl.BlockSpec((8, d), lambda i, j: (0, 0)),
            pl.BlockSpec((None, d, bn), lambda i, j: (i, 0, j)),
            pl.BlockSpec((None, 1, bn), lambda i, j: (i, 0, j)),
        ],
        out_specs=pl.BlockSpec((None, 8, bn), lambda i, j: (i, 0, j)),
        compiler_params=_params(2),
        name="ada",
    )(c8, w, b)


def _mixer_body(x_ref, halo_ref, mod_ref, par_ref, pw_ref, x1_ref, h2_ref, *, ts, group):
    i = pl.program_id(1)
    x = x_ref[...]
    g_a, g_f, pscale = par_ref[0:1, :], par_ref[1:2, :], par_ref[2:3, :]
    shift_a, scale_a, gate_a = mod_ref[0:1, :], mod_ref[1:2, :], mod_ref[2:3, :]
    shift_f, scale_f = mod_ref[3:4, :], mod_ref[4:5, :]

    h = _norm_mod(x, g_a, scale_a, shift_a)
    h_halo = jnp.where(i > 0, _norm_mod(halo_ref[...], g_a, scale_a, shift_a), 0.0)
    hh = jnp.concatenate([h_halo, h], axis=0)

    lt = min(V7X_LANES, group)
    pos = i * ts + lax.broadcasted_iota(jnp.int32, (ts, lt), 0)
    ys = []
    for g, k in enumerate(POOL_WINDOWS):
        sl = slice(g * group, (g + 1) * group)
        a = hh[:, sl]
        span = 1
        while span < k:
            a = a[:-span] + a[span:]
            span *= 2
        start = POOL_HALO + 1 - k
        win = a[start:start + ts]
        inv_cnt = 1.0 / jnp.minimum(pos + 1, k).astype(F32)
        hg = h[:, sl]
        p = jnp.concatenate([win[:, c:c + lt] * inv_cnt - hg[:, c:c + lt] for c in range(0, group, lt)], axis=1)
        ys.append(jnp.dot(p.astype(BF16), pw_ref[g], preferred_element_type=F32))
    y = jnp.concatenate(ys, axis=1) * pscale
    x1 = x + gate_a * y
    x1_ref[...] = x1
    h2_ref[...] = _norm_mod(x1, g_f, scale_f, shift_f).astype(BF16)


def _mixer(x, mod, par, pool_w):
    b, s, d = x.shape
    group = d // len(POOL_WINDOWS)
    ts = _pick(s, 256, POOL_HALO)
    hpb = ts // POOL_HALO
    return pl.pallas_call(
        functools.partial(_mixer_body, ts=ts, group=group),
        out_shape=(jax.ShapeDtypeStruct((b, s, d), F32), jax.ShapeDtypeStruct((b, s, d), BF16)),
        grid=(b, s // ts),
        in_specs=[
            pl.BlockSpec((None, ts, d), lambda bi, i: (bi, i, 0)),
            pl.BlockSpec((None, POOL_HALO, d), lambda bi, i: (bi, jnp.maximum(i * hpb - 1, 0), 0)),
            pl.BlockSpec((None, 8, d), lambda bi, i: (bi, 0, 0)),
            pl.BlockSpec((8, d), lambda bi, i: (0, 0)),
            pl.BlockSpec((len(POOL_WINDOWS), group, group), lambda bi, i: (0, 0, 0),
                         pipeline_mode=pl.Buffered(1)),
        ],
        out_specs=(pl.BlockSpec((None, ts, d), lambda bi, i: (bi, i, 0)),
                   pl.BlockSpec((None, ts, d), lambda bi, i: (bi, i, 0))),
        compiler_params=_params(2),
        name="mixer0",
    )(x, x, mod, par, pool_w)


def _norm2_body(x_ref, mod_ref, par_ref, a_ref, b_ref):
    x = x_ref[...]
    a_ref[...] = _norm_mod(x, par_ref[0:1, :], mod_ref[1:2, :], mod_ref[0:1, :]).astype(BF16)
    b_ref[...] = _norm_mod(x, par_ref[1:2, :], mod_ref[3:4, :], mod_ref[2:3, :]).astype(BF16)


def _norm2(x, mod, par):
    b, s, d = x.shape
    ts = _pick(s, 512, 16)
    row = pl.BlockSpec((None, ts, d), lambda bi, i: (bi, i, 0))
    return pl.pallas_call(
        _norm2_body,
        out_shape=(jax.ShapeDtypeStruct((b, s, d), BF16), jax.ShapeDtypeStruct((b, s, d), BF16)),
        grid=(b, s // ts),
        in_specs=[row, pl.BlockSpec((None, 8, d), lambda bi, i: (bi, 0, 0)),
                  pl.BlockSpec((8, d), lambda bi, i: (0, 0))],
        out_specs=(row, row),
        compiler_params=_params(2),
        name="norm_kv_q",
    )(x, mod, par)


def _route_body(x_ref, mod_ref, par_ref, rwh_ref, rwl_ref, hp_ref, route_ref, *, n_exp):
    x = x_ref[...]
    h = _norm_mod(x, par_ref[0:1, :], mod_ref[1:2, :], mod_ref[0:1, :])
    d = h.shape[1]
    bits = pltpu.bitcast(h.astype(BF16).astype(F32), jnp.uint32)
    packed = (bits[:, : d // 2] >> 16) | (bits[:, d // 2:] & jnp.uint32(0xFFFF0000))
    nseg = d // 2 // V7X_LANES
    ts = h.shape[0]
    for a in range(nseg):
        hp_ref[pl.ds(a, ts, stride=nseg), :] = packed[:, a * V7X_LANES:(a + 1) * V7X_LANES]
    h_hi = h.astype(BF16)
    h_lo = (h - h_hi.astype(F32)).astype(BF16)
    logits = (jnp.dot(h_hi, rwh_ref[...], preferred_element_type=F32)
              + jnp.dot(h_hi, rwl_ref[...], preferred_element_type=F32)
              + jnp.dot(h_lo, rwh_ref[...], preferred_element_type=F32))
    lane = lax.broadcasted_iota(jnp.int32, logits.shape, 1)
    lane_f = lane.astype(F32)
    lg = jnp.where(lane < n_exp, logits, -jnp.inf)
    v1 = jnp.max(lg, axis=1, keepdims=True)
    i1 = jnp.min(jnp.where(lg == v1, lane_f, float(V7X_LANES)), axis=1, keepdims=True)
    lg2 = jnp.where(lane_f == i1, -jnp.inf, lg)
    v2 = jnp.max(lg2, axis=1, keepdims=True)
    i2 = jnp.min(jnp.where(lg2 == v2, lane_f, float(V7X_LANES)), axis=1, keepdims=True)
    e2 = jnp.exp(v2 - v1)
    g1 = 1.0 / (1.0 + e2)
    g2 = e2 / (1.0 + e2)
    out = jnp.where(lane == 0, i1, 0.0)
    out = jnp.where(lane == 1, i2, out)
    out = jnp.where(lane == 2, g1, out)
    out = jnp.where(lane == 3, g2, out)
    route_ref[...] = out


def _route(x, mod, par, rw_pad, n_exp):
    b, s, d = x.shape
    rw_hi = rw_pad.astype(BF16)
    rw_hi_residual = (rw_pad - rw_hi.astype(F32)).astype(BF16)
    ts = _pick(s, 512, 16)
    nseg = d // 2 // V7X_LANES
    tpb = s // ts
    return pl.pallas_call(
        functools.partial(_route_body, n_exp=n_exp),
        out_shape=(jax.ShapeDtypeStruct((b * s * nseg, V7X_LANES), jnp.uint32),
                   jax.ShapeDtypeStruct((b, s, V7X_LANES), F32)),
        grid=(b, tpb),
        in_specs=[pl.BlockSpec((None, ts, d), lambda bi, i: (bi, i, 0)),
                  pl.BlockSpec((None, 8, d), lambda bi, i: (bi, 0, 0)),
                  pl.BlockSpec((8, d), lambda bi, i: (0, 0)),
                  pl.BlockSpec((d, V7X_LANES), lambda bi, i: (0, 0)),
                  pl.BlockSpec((d, V7X_LANES), lambda bi, i: (0, 0))],
        out_specs=(pl.BlockSpec((ts * nseg, V7X_LANES), lambda bi, i: (bi * tpb + i, 0)),
                   pl.BlockSpec((None, ts, V7X_LANES), lambda bi, i: (bi, i, 0))),
        compiler_params=_params(2),
        name="norm_route",
    )(x, mod, par, rw_hi, rw_hi_residual)


def _weights_changed(te_ref, i):
    return (i == 0) | (te_ref[i] != te_ref[jnp.maximum(i - 1, 0)])


def _up_body(te_ref, nv_ref, a_ref, w1_ref, w3_ref, o_ref, w1b, w3b):
    i = pl.program_id(1)

    @pl.when(_weights_changed(te_ref, i))
    def _():
        w1b[...] = w1_ref[...].astype(BF16)
        w3b[...] = w3_ref[...].astype(BF16)

    @pl.when(i < nv_ref[0])
    def _():
        a = a_ref[...]
        h1 = jnp.dot(a, w1b[...], preferred_element_type=F32)
        h3 = jnp.dot(a, w3b[...], preferred_element_type=F32)
        o_ref[...] = (_silu(h1) * h3).astype(BF16)

    @pl.when(i >= nv_ref[0])
    def _():
        o_ref[...] = jnp.zeros_like(o_ref)


def _swiglu_up(tile_expert, n_valid, a, w1, w3, tm, bn_pref, name):
    m, k = a.shape
    f = w1.shape[2]
    bn = _pick(f, bn_pref, V7X_LANES)
    gs = pltpu.PrefetchScalarGridSpec(
        num_scalar_prefetch=2,
        grid=(f // bn, m // tm),
        in_specs=[pl.BlockSpec((tm, k), lambda j, i, te, nv: (jnp.minimum(i, nv[0] - 1), 0)),
                  pl.BlockSpec((None, k, bn), lambda j, i, te, nv: (te[i], 0, j)),
                  pl.BlockSpec((None, k, bn), lambda j, i, te, nv: (te[i], 0, j))],
        out_specs=pl.BlockSpec((tm, bn), lambda j, i, te, nv: (i, j)),
        scratch_shapes=[pltpu.VMEM((k, bn), BF16), pltpu.VMEM((k, bn), BF16)],
    )
    return pl.pallas_call(
        _up_body,
        out_shape=jax.ShapeDtypeStruct((m, f), BF16),
        grid_spec=gs,
        compiler_params=_params(2),
        name=name,
    )(tile_expert, n_valid, a, w1, w3)


def _down_res_body(a_ref, w_ref, x_ref, mod_ref, o_ref, *, gate_row):
    y = jnp.dot(a_ref[...], w_ref[...].astype(BF16), preferred_element_type=F32)
    o_ref[...] = x_ref[...] + mod_ref[gate_row:gate_row + 1, :] * y


def _matmul_residual(a, w, xres, mod, gate_row, tm_pref, bn_pref, name):
    m, k = a.shape
    n = w.shape[1]
    b = mod.shape[0]
    s = m // b
    tm = _pick(s, tm_pref, 16)
    bn = _pick(n, bn_pref, V7X_LANES)
    return pl.pallas_call(
        functools.partial(_down_res_body, gate_row=gate_row),
        out_shape=jax.ShapeDtypeStruct((m, n), F32),
        grid=(m // tm, n // bn),
        in_specs=[pl.BlockSpec((tm, k), lambda i, j: (i, 0)),
                  pl.BlockSpec((k, bn), lambda i, j: (0, j)),
                  pl.BlockSpec((tm, bn), lambda i, j: (i, j)),
                  pl.BlockSpec((None, 8, bn), lambda i, j: ((i * tm) // s, 0, j))],
        out_specs=pl.BlockSpec((tm, bn), lambda i, j: (i, j)),
        compiler_params=_params(2),
        name=name,
    )(a, w, xres, mod)


def _down_grouped_body(te_ref, nv_ref, a_ref, w_ref, o_ref, wb):
    i = pl.program_id(1)

    @pl.when(_weights_changed(te_ref, i))
    def _():
        wb[...] = w_ref[...].astype(BF16)

    @pl.when(i < nv_ref[0])
    def _():
        o_ref[...] = jnp.dot(a_ref[...], wb[...], preferred_element_type=F32)

    @pl.when(i >= nv_ref[0])
    def _():
        o_ref[...] = jnp.zeros_like(o_ref)


def _down_grouped(tile_expert, n_valid, a, w, tm):
    m, k = a.shape
    n = w.shape[2]
    bn = _pick(n, 1024, V7X_LANES)
    gs = pltpu.PrefetchScalarGridSpec(
        num_scalar_prefetch=2,
        grid=(n // bn, m // tm),
        in_specs=[pl.BlockSpec((tm, k), lambda j, i, te, nv: (jnp.minimum(i, nv[0] - 1), 0)),
                  pl.BlockSpec((None, k, bn), lambda j, i, te, nv: (te[i], 0, j))],
        out_specs=pl.BlockSpec((tm, bn), lambda j, i, te, nv: (i, j)),
        scratch_shapes=[pltpu.VMEM((k, bn), BF16)],
    )
    return pl.pallas_call(
        _down_grouped_body,
        out_shape=jax.ShapeDtypeStruct((m, n), F32),
        grid_spec=gs,
        compiler_params=_params(2),
        name="moe_down",
    )(tile_expert, n_valid, a, w)


def _proj_body(a_ref, w_ref, g_ref, *rest, dils, normed, out_scale, tm, n_row_tiles):
    o_refs, (ybuf, qbuf, wb) = rest[:len(dils)], rest[len(dils):]
    s = pl.program_id(0)
    n_slab = ybuf.shape[0]

    @pl.when(s == 0)
    def _():
        ybuf[...] = jnp.zeros_like(ybuf)

    @pl.when(s % n_row_tiles == 0)
    def _():
        wb[...] = w_ref[...].astype(BF16)

    gvec = g_ref[...] * out_scale

    def head_norm(ys):
        if not normed:
            return ys
        ms = jnp.mean(ys * ys, axis=-1, keepdims=True)
        return (ys * lax.rsqrt(ms + EPS)) * gvec

    out = dict(zip(dils, o_refs))
    quarter = tm // 4
    for sl in range(n_slab):
        if 1 in out:
            out[1][sl, 0] = head_norm(ybuf[sl]).astype(BF16)
        if 4 in out or 16 in out:
            for r4 in range(4):
                ys = head_norm(ybuf[sl, pl.ds(r4, quarter, stride=4), :])
                if 4 in out:
                    out[4][sl, r4] = ys.astype(BF16)
                if 16 in out:
                    qbuf[sl, pl.ds(r4 * quarter, quarter), :] = ys
            if 16 in out:
                for r4 in range(4):
                    for hi in range(4):
                        out[16][sl, r4 + 4 * hi] = qbuf[sl, pl.ds(r4 * quarter + hi, tm // 16, stride=4), :].astype(BF16)

    y = jnp.dot(a_ref[...], wb[...], preferred_element_type=F32)
    for sl in range(n_slab):
        ybuf[sl] = y[:, sl * HEAD_DIM:(sl + 1) * HEAD_DIM]


def _proj_heads(a, w, col0, gvec, dils, normed, out_scale, batch, name):
    m, k = a.shape
    s = m // batch
    d_model = k
    n_heads = d_model // HEAD_DIM
    tm = _pick(s, 1024, 16 * max(dils))
    bn = _pick(d_model, 512, HEAD_DIM)
    hpb = bn // HEAD_DIM
    tpb = s // tm
    ni = m // tm
    n_steps = ni * (d_model // bn)
    cb0 = col0 // bn
    cur = lambda t: jnp.minimum(t, n_steps - 1)
    prev = lambda t: jnp.maximum(t - 1, 0)
    outs = tuple(jax.ShapeDtypeStruct((batch, n_heads, d, s // d, HEAD_DIM), BF16) for d in dils)
    out_specs = tuple(
        pl.BlockSpec((None, hpb, d, tm // d, HEAD_DIM),
                     lambda t: ((prev(t) % ni) // tpb, prev(t) // ni, 0, (prev(t) % ni) % tpb, 0)) for d in dils)
    return pl.pallas_call(
        functools.partial(_proj_body, dils=dils, normed=normed, out_scale=out_scale, tm=tm, n_row_tiles=ni),
        out_shape=outs,
        grid=(n_steps + 1,),
        in_specs=[pl.BlockSpec((tm, k), lambda t: (cur(t) % ni, 0)),
                  pl.BlockSpec((k, bn), lambda t: (0, cb0 + cur(t) // ni)),
                  pl.BlockSpec((1, HEAD_DIM), lambda t: (0, 0))],
        out_specs=out_specs,
        scratch_shapes=[pltpu.VMEM((hpb, tm, HEAD_DIM), F32), pltpu.VMEM((hpb, tm, HEAD_DIM), F32),
                        pltpu.VMEM((k, bn), BF16)],
        compiler_params=_params(1),
        name=name,
    )(a, w, gvec)


def _band_buckets():
    i = np.arange(BLOCK)[:, None]
    kk = np.arange(2 * BLOCK)[None, :]
    rel = i + BLOCK - kk
    valid = (rel >= 0) & (rel <= N_TAPS)
    max_exact = NUM_BUCKETS // 2
    out = []
    for d in DILATIONS:
        n = np.maximum(rel, 0) * d
        nf = np.maximum(n, 1).astype(np.float32)
        large = max_exact + (np.log(nf / np.float32(max_exact)) / np.float32(math.log(MAX_DISTANCE / max_exact))
                             * np.float32(NUM_BUCKETS - max_exact)).astype(np.int32)
        bucket = np.where(n < max_exact, n, np.minimum(large, NUM_BUCKETS - 1))
        out.append(np.where(valid, bucket, -1))
    return np.stack(out).astype(np.int32)


def _bias_body(rb_ref, bk_ref, o_ref, *, n_heads, heads_per_step):
    g = pl.program_id(0)
    h0 = pl.program_id(1) * heads_per_step
    bucket = bk_ref[...]
    for hh in range(heads_per_step):
        acc = jnp.full(bucket.shape, NEG_INF, F32)
        for b in range(NUM_BUCKETS):
            acc = jnp.where(bucket == b, rb_ref[(b * len(DILATIONS) + g) * n_heads + h0 + hh], acc)
        o_ref[hh] = acc


def _bias_bands(rel_bias, n_heads):
    buckets = jnp.asarray(_band_buckets())
    n_g = len(DILATIONS)
    hps = _pick(n_heads, 8, 1)
    return pl.pallas_call(
        functools.partial(_bias_body, n_heads=n_heads, heads_per_step=hps),
        out_shape=jax.ShapeDtypeStruct((n_g, n_heads, BLOCK, 2 * BLOCK), F32),
        grid=(n_g, n_heads // hps),
        in_specs=[pl.BlockSpec(memory_space=pltpu.SMEM),
                  pl.BlockSpec((None, BLOCK, 2 * BLOCK), lambda g, h: (g, 0, 0))],
        out_specs=pl.BlockSpec((None, hps, BLOCK, 2 * BLOCK), lambda g, h: (g, h, 0, 0)),
        compiler_params=_params(2),
        name="bias_bands",
    )(rel_bias.reshape(-1), buckets)


def _attn_body(*refs, hb):
    n_g = len(DILATIONS)
    q_refs = refs[0:n_g]
    k_refs = refs[n_g:3 * n_g:2]
    kh_refs = refs[n_g + 1:3 * n_g:2]
    v_refs = refs[3 * n_g:5 * n_g:2]
    vh_refs = refs[3 * n_g + 1:5 * n_g:2]
    bias_ref = refs[5 * n_g]
    o_ref = refs[5 * n_g + 1]
    scr = refs[5 * n_g + 2:]
    kcats, vcats = scr[0:n_g], scr[n_g:2 * n_g]
    bsc, oscr, mscr, lscr = scr[2 * n_g:2 * n_g + 4]

    ti = pl.program_id(2)
    col = lax.broadcasted_iota(jnp.int32, (BLOCK, 2 * BLOCK), 1)
    pen = jnp.where((col < BLOCK) & (ti == 0), NEG_INF, 0.0)

    for h in range(hb):
        for g in range(n_g):
            band = bias_ref[g, h]
            bsc[0, g] = band
            bsc[1, g] = band + pen

        for g, d in enumerate(DILATIONS):
            n = ATTN_TILE // d
            nj = n // BLOCK
            kcat, vcat = kcats[g], vcats[g]
            kcat[:, 0:BLOCK, :] = kh_refs[g][h]
            kcat[:, BLOCK:, :] = k_refs[g][h]
            vcat[:, 0:BLOCK, :] = vh_refs[g][h]
            vcat[:, BLOCK:, :] = v_refs[g][h]
            q_ref = q_refs[g]

            def unit(u, carry, g=g, d=d, nj=nj, kcat=kcat, vcat=vcat, q_ref=q_ref, h=h):
                r = u // nj
                j = u % nj
                row = pl.multiple_of(j * BLOCK, BLOCK)
                q = q_ref[h, r, pl.ds(row, BLOCK), :]
                kk = kcat[r, pl.ds(row, 2 * BLOCK), :]
                vv = vcat[r, pl.ds(row, 2 * BLOCK), :]
                s = lax.dot_general(q, kk, (((1,), (1,)), ((), ())), preferred_element_type=F32)
                s = s + bsc[jnp.where(j == 0, 1, 0), g]
                m = jnp.max(s, axis=1, keepdims=True)
                p = jnp.exp(s - m)
                l = jnp.sum(p, axis=1, keepdims=True)
                acc = jnp.dot(p.astype(BF16), vv, preferred_element_type=F32)
                start = j * (BLOCK * d) + r
                if d == 1:
                    rows = pl.ds(pl.multiple_of(start, BLOCK), BLOCK)
                else:
                    rows = pl.ds(start, BLOCK, stride=d)
                oscr[g, rows, :] = acc
                mscr[g, rows, :] = jnp.broadcast_to(m, (BLOCK, HEAD_DIM))
                lscr[g, rows, :] = jnp.broadcast_to(l, (BLOCK, HEAD_DIM))
                return carry

            lax.fori_loop(0, d * nj, unit, 0, unroll=ATTN_UNROLL)

        chunk = 2 * BLOCK

        def merge(c, carry, h=h):
            rows = pl.ds(pl.multiple_of(c * chunk, chunk), chunk)
            ms = [mscr[g, rows, :] for g in range(n_g)]
            mx = functools.reduce(jnp.maximum, ms)
            es = [jnp.exp(m - mx) for m in ms]
            num = sum(e * oscr[g, rows, :] for g, e in enumerate(es))
            den = sum(e * lscr[g, rows, :] for g, e in enumerate(es))
            o_ref[rows, h * HEAD_DIM:(h + 1) * HEAD_DIM] = (num * (1.0 / den)).astype(BF16)
            return carry

        lax.fori_loop(0, ATTN_TILE // chunk, merge, 0)


def _attention(qs, ks, vs, bias, batch, seq, n_heads):
    n_g = len(DILATIONS)
    hb = ATTN_HEADS if n_heads % ATTN_HEADS == 0 else 1
    in_specs, args = [], []
    for g, d in enumerate(DILATIONS):
        n = ATTN_TILE // d
        in_specs.append(pl.BlockSpec((None, hb, d, n, HEAD_DIM), lambda b, h, t: (b, h, 0, t, 0)))
        args.append(qs[g])
    for src in (ks, vs):
        for g, d in enumerate(DILATIONS):
            n = ATTN_TILE // d
            nb = n // BLOCK
            in_specs.append(pl.BlockSpec((None, hb, d, n, HEAD_DIM), lambda b, h, t: (b, h, 0, t, 0)))
            in_specs.append(pl.BlockSpec((None, hb, d, BLOCK, HEAD_DIM),
                                         lambda b, h, t, nb=nb: (b, h, 0, jnp.maximum(t * nb - 1, 0), 0)))
            args += [src[g], src[g]]
    in_specs.append(pl.BlockSpec((n_g, hb, BLOCK, 2 * BLOCK), lambda b, h, t: (0, h, 0, 0)))
    args.append(bias)
    scratch = [pltpu.VMEM((d, BLOCK + ATTN_TILE // d, HEAD_DIM), BF16) for d in DILATIONS] * 2
    scratch += [pltpu.VMEM((2, n_g, BLOCK, 2 * BLOCK), F32)]
    scratch += [pltpu.VMEM((n_g, ATTN_TILE, HEAD_DIM), F32)] * 3
    return pl.pallas_call(
        functools.partial(_attn_body, hb=hb),
        out_shape=jax.ShapeDtypeStruct((batch, seq, n_heads * HEAD_DIM), BF16),
        grid=(batch, n_heads // hb, seq // ATTN_TILE),
        in_specs=in_specs,
        out_specs=pl.BlockSpec((None, ATTN_TILE, hb * HEAD_DIM), lambda b, h, t: (b, t, h)),
        scratch_shapes=scratch,
        compiler_params=_params(3),
        name="dilated_attn",
    )(*args)


def _gather_body(nrows_ref, idx_ref, idxn_ref, src_ref, o_ref, buf, sem, *, rb, nseg):
    i = pl.program_id(0)
    n = pl.num_programs(0)
    holds_rows = lambda blk: blk * rb < nrows_ref[0]

    def start_block(ids_ref, slot):
        def issue(r, c):
            pltpu.make_async_copy(src_ref.at[pl.ds(ids_ref[0, r] * nseg, nseg)],
                                  buf.at[slot, pl.ds(r * nseg, nseg)], sem.at[slot]).start()
            return c
        lax.fori_loop(0, rb, issue, 0, unroll=8)

    @pl.when((i == 0) & holds_rows(0))
    def _():
        start_block(idx_ref, 0)

    @pl.when((i + 1 < n) & holds_rows(i + 1))
    def _():
        start_block(idxn_ref, (i + 1) % 2)

    half = nseg * V7X_LANES
    for slot in range(2):
        @pl.when(holds_rows(i) & (i % 2 == slot))
        def _(slot=slot):
            pltpu.make_async_copy(src_ref.at[pl.ds(0, rb * nseg)], buf.at[slot], sem.at[slot]).wait()
            for a in range(nseg):
                u = buf[slot, pl.ds(a, rb, stride=nseg), :]
                o_ref[:, a * V7X_LANES:(a + 1) * V7X_LANES] = pltpu.bitcast(u << 16, F32).astype(BF16)
                o_ref[:, half + a * V7X_LANES:half + (a + 1) * V7X_LANES] = (
                    pltpu.bitcast(u & jnp.uint32(0xFFFF0000), F32).astype(BF16))

    @pl.when(jnp.logical_not(holds_rows(i)))
    def _():
        o_ref[...] = jnp.zeros_like(o_ref)


def _gather_rows(n_rows_valid, src_idx, hp, d, rb):
    n_rows = src_idx.shape[0]
    nseg = d // 2 // V7X_LANES
    nblk = n_rows // rb
    idx3 = src_idx.reshape(nblk, 1, rb)
    gs = pltpu.PrefetchScalarGridSpec(
        num_scalar_prefetch=1,
        grid=(nblk,),
        in_specs=[pl.BlockSpec((None, 1, rb), lambda i, nr: (i, 0, 0), memory_space=pltpu.SMEM),
                  pl.BlockSpec((None, 1, rb), lambda i, nr: (jnp.minimum(i + 1, nblk - 1), 0, 0),
                               memory_space=pltpu.SMEM),
                  pl.BlockSpec(memory_space=pl.ANY)],
        out_specs=pl.BlockSpec((rb, d), lambda i, nr: (i, 0)),
        scratch_shapes=[pltpu.VMEM((2, rb * nseg, V7X_LANES), jnp.uint32), pltpu.SemaphoreType.DMA((2,))],
    )
    return pl.pallas_call(
        functools.partial(_gather_body, rb=rb, nseg=nseg),
        out_shape=jax.ShapeDtypeStruct((n_rows, d), BF16),
        grid_spec=gs,
        compiler_params=_params(1),
        name="moe_gather",
    )(n_rows_valid, idx3, idx3, hp)


def _combine_body(pos_ref, posn_ref, y_ref, x_ref, route_ref, mod_ref, o_ref, buf, sem, *, tc, gate_row):
    i = pl.program_id(0)
    n = pl.num_programs(0)

    def start_block(p_ref, slot):
        def issue(r, c):
            pltpu.make_async_copy(y_ref.at[pl.ds(p_ref[0, r], 1)], buf.at[slot, 0, pl.ds(r, 1)], sem.at[slot]).start()
            pltpu.make_async_copy(y_ref.at[pl.ds(p_ref[0, tc + r], 1)], buf.at[slot, 1, pl.ds(r, 1)],
                                  sem.at[slot]).start()
            return c
        lax.fori_loop(0, tc, issue, 0, unroll=8)

    @pl.when(i == 0)
    def _():
        start_block(pos_ref, 0)

    @pl.when(i + 1 < n)
    def _():
        start_block(posn_ref, (i + 1) % 2)

    g1 = route_ref[:, 2:3]
    g2 = route_ref[:, 3:4]
    for slot in range(2):
        @pl.when(i % 2 == slot)
        def _(slot=slot):
            pltpu.make_async_copy(y_ref.at[pl.ds(0, tc)], buf.at[slot, 0], sem.at[slot]).wait()
            pltpu.make_async_copy(y_ref.at[pl.ds(0, tc)], buf.at[slot, 1], sem.at[slot]).wait()
            y = g1 * buf[slot, 0] + g2 * buf[slot, 1]
            o_ref[...] = x_ref[...] + mod_ref[gate_row:gate_row + 1, :] * y


def _combine(pos, y, x, route, mod, gate_row):
    m, d = x.shape
    b = mod.shape[0]
    s = m // b
    tc = pos.shape[2] // 2
    nblk = m // tc
    return pl.pallas_call(
        functools.partial(_combine_body, tc=tc, gate_row=gate_row),
        out_shape=jax.ShapeDtypeStruct((m, d), F32),
        grid=(nblk,),
        in_specs=[pl.BlockSpec((None, 1, 2 * tc), lambda i: (i, 0, 0), memory_space=pltpu.SMEM),
                  pl.BlockSpec((None, 1, 2 * tc), lambda i: (jnp.minimum(i + 1, nblk - 1), 0, 0),
                               memory_space=pltpu.SMEM),
                  pl.BlockSpec(memory_space=pl.ANY),
                  pl.BlockSpec((tc, d), lambda i: (i, 0)),
                  pl.BlockSpec((tc, V7X_LANES), lambda i: (i, 0)),
                  pl.BlockSpec((None, 8, d), lambda i: ((i * tc) // s, 0, 0))],
        out_specs=pl.BlockSpec((tc, d), lambda i: (i, 0)),
        scratch_shapes=[pltpu.VMEM((2, 2, tc, d), F32), pltpu.SemaphoreType.DMA((2,))],
        compiler_params=_params(1),
        name="moe_combine",
    )(pos, pos, y, x, route, mod)


def _rows8(*rows):
    b, d = rows[0].shape
    pad = [jnp.zeros((b, d), F32)] * (8 - len(rows))
    return jnp.stack(list(rows) + pad, axis=1)


def _par8(*rows):
    d = rows[0].shape[0]
    pad = [jnp.zeros((d,), F32)] * (8 - len(rows))
    return jnp.stack(list(rows) + pad, axis=0)


def kernel(x, c, ada_w, ada_b, norm_g, pool_w, pool_scale, kv_ada_w, kv_ada_b, kv_norm_g, w_k, w_v, k_norm_g,
           w_q, q_norm_g, w_o, rel_bias, ffn_w1, ffn_w3, ffn_w2, router_w, moe_w1, moe_w3, moe_w2):
    batch, seq, d = x.shape
    n_tok = batch * seq
    n_heads = d // HEAD_DIM
    n_exp = router_w.shape[2]
    assert d % HEAD_DIM == 0 and seq % ATTN_TILE == 0 and batch <= 8

    c8 = jnp.pad(c, ((0, 8 - batch), (0, 0)))
    mods = _ada(c8, ada_w.reshape(4, d, 3 * d), ada_b.reshape(4, 1, 3 * d))[:, :batch]
    kvm = _ada(c8, kv_ada_w[None], kv_ada_b[None, None])[0, :batch]
    sh = lambda m: (mods[m, :, :d], mods[m, :, d:2 * d], mods[m, :, 2 * d:])
    (s00, c00, g00), (s01, c01, g01), (s10, c10, g10), (s11, c11, g11) = sh(0), sh(1), sh(2), sh(3)
    kv_shift, kv_scale = kvm[:, :d], kvm[:, d:]

    x1, h2 = _mixer(x, _rows8(s00, c00, g00, s01, c01), _par8(norm_g[0, 0], norm_g[0, 1], pool_scale[0]),
                    pool_w[0].astype(BF16))
    tm_ffn = _pick(n_tok, 2048, 16)
    gact = _swiglu_up(jnp.zeros((n_tok // tm_ffn,), jnp.int32), jnp.full((1,), n_tok // tm_ffn, jnp.int32),
                      h2.reshape(n_tok, d), ffn_w1[:1], ffn_w3[:1], tm_ffn, 256, "ffn_up")
    x2 = _matmul_residual(gact, ffn_w2[0].astype(BF16), x1.reshape(n_tok, d), _rows8(g01), 0, 512, 512, "ffn_down")

    kvh, h3 = _norm2(x2.reshape(batch, seq, d), _rows8(kv_shift, kv_scale, s10, c10), _par8(kv_norm_g, norm_g[1, 0]))
    kvh, h3 = kvh.reshape(n_tok, d), h3.reshape(n_tok, d)
    ks = _proj_heads(kvh, w_k, 0, k_norm_g.reshape(1, HEAD_DIM), DILATIONS, True, 1.0, batch, "proj_k")
    vs = _proj_heads(kvh, w_v, 0, k_norm_g.reshape(1, HEAD_DIM), DILATIONS, False, 1.0, batch, "proj_v")
    qs = [_proj_heads(h3, w_q[0], g * d, q_norm_g[0].reshape(1, HEAD_DIM), (dl,), True, HEAD_DIM ** -0.5, batch,
                      f"proj_q{dl}")[0] for g, dl in enumerate(DILATIONS)]

    bias = _bias_bands(rel_bias, n_heads)
    att = _attention(qs, ks, vs, bias, batch, seq, n_heads)
    x3 = _matmul_residual(att.reshape(n_tok, d), w_o[0].astype(BF16), x2, _rows8(g10), 0, 1024, 512, "attn_out")

    rw_pad = jnp.pad(router_w[0], ((0, 0), (0, V7X_LANES - n_exp)))
    hp, route = _route(x3.reshape(batch, seq, d), _rows8(s11, c11), _par8(norm_g[1, 1]), rw_pad, n_exp)
    route = route.reshape(n_tok, V7X_LANES)

    tm = _pick(n_tok, 512, 256)
    r_pad = TOP_K * n_tok + n_exp * tm
    e_flat = jnp.concatenate([route[:, 0], route[:, 1]]).astype(jnp.int32)
    onehot = (e_flat[:, None] == jnp.arange(n_exp, dtype=jnp.int32)[None, :]).astype(jnp.int32)
    csum = jnp.cumsum(onehot, axis=0)
    rank = jnp.sum((csum - 1) * onehot, axis=1)
    padded = ((csum[-1] + tm - 1) // tm) * tm
    gend = jnp.cumsum(padded)
    dest = (gend - padded)[e_flat] + rank
    tok = jnp.arange(TOP_K * n_tok, dtype=jnp.int32) % n_tok
    src_idx = jnp.zeros((r_pad,), jnp.int32).at[dest].set(tok)
    tile_expert = jnp.minimum(
        jnp.searchsorted(gend, jnp.arange(r_pad // tm, dtype=jnp.int32) * tm, side="right"), n_exp - 1
    ).astype(jnp.int32)

    n_valid = (gend[-1:] // tm).astype(jnp.int32)

    xg = _gather_rows(gend[-1:].astype(jnp.int32), src_idx, hp, d, tm)
    hg = _swiglu_up(tile_expert, n_valid, xg, moe_w1[0], moe_w3[0], tm, 512, "moe_up")
    yg = _down_grouped(tile_expert, n_valid, hg, moe_w2[0], tm)

    tc = _pick(n_tok, 256, 8)
    pos = jnp.concatenate([dest[:n_tok].reshape(n_tok // tc, 1, tc), dest[n_tok:].reshape(n_tok // tc, 1, tc)], axis=2)
    out = _combine(pos.astype(jnp.int32), yg, x3, route, _rows8(g11), 0)
    return out.reshape(batch, seq, d)
```

```python
import functools
import math

import numpy as np
import jax
import jax.numpy as jnp
from jax import lax
from jax.experimental import pallas as pl
from jax.experimental.pallas import tpu as pltpu

F32 = jnp.float32
BF16 = jnp.bfloat16

V7X_LANES = 128
V7X_VMEM_LIMIT_BYTES = 60 * 1024 * 1024

HEAD_DIM = 128
BLOCK = 128
DILATIONS = (1, 4, 16)
N_TAPS = 128
NUM_BUCKETS = 32
MAX_DISTANCE = 2048
POOL_WINDOWS = (2, 4, 8, 16)
POOL_HALO = 16
TOP_K = 2
EPS = 1e-6
NEG_INF = -1e30
ATTN_TILE = BLOCK * DILATIONS[-1]
ATTN_UNROLL = 16
ATTN_HEADS = 2


def _params(n_grid, vmem=V7X_VMEM_LIMIT_BYTES):
    return pltpu.CompilerParams(dimension_semantics=("arbitrary",) * n_grid, vmem_limit_bytes=vmem)


def _pick(n, pref, mult):
    if n <= pref:
        return n
    t = (pref // mult) * mult
    while t >= mult:
        if n % t == 0:
            return t
        t -= mult
    return n


def _silu(v):
    return v * jax.nn.sigmoid(v)


def _norm_mod(v, g, scale, shift):
    ms = jnp.mean(v * v, axis=-1, keepdims=True)
    return (v * lax.rsqrt(ms + EPS)) * (g * (1.0 + scale)) + shift


def _ada_body(c_ref, w_ref, b_ref, o_ref):
    cond = _silu(c_ref[...]).astype(BF16)
    o_ref[...] = jnp.dot(cond, w_ref[...].astype(BF16), preferred_element_type=F32) + b_ref[...]


def _ada(c8, w, b):
    m, d, n = w.shape
    bn = _pick(n, 512, V7X_LANES)
    return pl.pallas_call(
        _ada_body,
        out_shape=jax.ShapeDtypeStruct((m, 8, n), F32),
        grid=(m, n // bn),
        in_specs=[
            pl.BlockSpec((8, d), lambda i, j: (0, 0)),
            pl.BlockSpec((None, d, bn), lambda i, j: (i, 0, j)),
            pl.BlockSpec((None, 1, bn), lambda i, j: (i, 0, j)),
        ],
        out_specs=pl.BlockSpec((None, 8, bn), lambda i, j: (i, 0, j)),
        compiler_params=_params(2),
        name="ada",
    )(c8, w, b)


def _mixer_body(x_ref, halo_ref, mod_ref, par_ref, pw_ref, x1_ref, h2_ref, *, ts, group):
    i = pl.program_id(1)
    x = x_ref[...]
    g_a, g_f, pscale = par_ref[0:1, :], par_ref[1:2, :], par_ref[2:3, :]
    shift_a, scale_a, gate_a = mod_ref[0:1, :], mod_ref[1:2, :], mod_ref[2:3, :]
    shift_f, scale_f = mod_ref[3:4, :], mod_ref[4:5, :]

    h = _norm_mod(x, g_a, scale_a, shift_a)
    h_halo = jnp.where(i > 0, _norm_mod(halo_ref[...], g_a, scale_a, shift_a), 0.0)
    hh = jnp.concatenate([h_halo, h], axis=0)

    lt = min(V7X_LANES, group)
    pos = i * ts + lax.broadcasted_iota(jnp.int32, (ts, lt), 0)
    ys = []
    for g, k in enumerate(POOL_WINDOWS):
        sl = slice(g * group, (g + 1) * group)
        a = hh[:, sl]
        span = 1
        while span < k:
            a = a[:-span] + a[span:]
            span *= 2
        start = POOL_HALO + 1 - k
        win = a[start:start + ts]
        inv_cnt = 1.0 / jnp.minimum(pos + 1, k).astype(F32)
        hg = h[:, sl]
        p = jnp.concatenate([win[:, c:c + lt] * inv_cnt - hg[:, c:c + lt] for c in range(0, group, lt)], axis=1)
        ys.append(jnp.dot(p.astype(BF16), pw_ref[g], preferred_element_type=F32))
    y = jnp.concatenate(ys, axis=1) * pscale
    x1 = x + gate_a * y
    x1_ref[...] = x1
    h2_ref[...] = _norm_mod(x1, g_f, scale_f, shift_f).astype(BF16)


def _mixer(x, mod, par, pool_w):
    b, s, d = x.shape
    group = d // len(POOL_WINDOWS)
    ts = _pick(s, 256, POOL_HALO)
    hpb = ts // POOL_HALO
    return pl.pallas_call(
        functools.partial(_mixer_body, ts=ts, group=group),
        out_shape=(jax.ShapeDtypeStruct((b, s, d), F32), jax.ShapeDtypeStruct((b, s, d), BF16)),
        grid=(b, s // ts),
        in_specs=[
            pl.BlockSpec((None, ts, d), lambda bi, i: (bi, i, 0)),
            pl.BlockSpec((None, POOL_HALO, d), lambda bi, i: (bi, jnp.maximum(i * hpb - 1, 0), 0)),
            pl.BlockSpec((None, 8, d), lambda bi, i: (bi, 0, 0)),
            pl.BlockSpec((8, d), lambda bi, i: (0, 0)),
            pl.BlockSpec((len(POOL_WINDOWS), group, group), lambda bi, i: (0, 0, 0),
                         pipeline_mode=pl.Buffered(1)),
        ],
        out_specs=(pl.BlockSpec((None, ts, d), lambda bi, i: (bi, i, 0)),
                   pl.BlockSpec((None, ts, d), lambda bi, i: (bi, i, 0))),
        compiler_params=_params(2),
        name="mixer0",
    )(x, x, mod, par, pool_w)


def _norm2_body(x_ref, mod_ref, par_ref, a_ref, b_ref):
    x = x_ref[...]
    a_ref[...] = _norm_mod(x, par_ref[0:1, :], mod_ref[1:2, :], mod_ref[0:1, :]).astype(BF16)
    b_ref[...] = _norm_mod(x, par_ref[1:2, :], mod_ref[3:4, :], mod_ref[2:3, :]).astype(BF16)


def _norm2(x, mod, par):
    b, s, d = x.shape
    ts = _pick(s, 512, 16)
    row = pl.BlockSpec((None, ts, d), lambda bi, i: (bi, i, 0))
    return pl.pallas_call(
        _norm2_body,
        out_shape=(jax.ShapeDtypeStruct((b, s, d), BF16), jax.ShapeDtypeStruct((b, s, d), BF16)),
        grid=(b, s // ts),
        in_specs=[row, pl.BlockSpec((None, 8, d), lambda bi, i: (bi, 0, 0)),
                  pl.BlockSpec((8, d), lambda bi, i: (0, 0))],
        out_specs=(row, row),
        compiler_params=_params(2),
        name="norm_kv_q",
    )(x, mod, par)


def _route_body(x_ref, mod_ref, par_ref, rwh_ref, rwl_ref, hp_ref, route_ref, *, n_exp):
    x = x_ref[...]
    h = _norm_mod(x, par_ref[0:1, :], mod_ref[1:2, :], mod_ref[0:1, :])
    d = h.shape[1]
    bits = pltpu.bitcast(h.astype(BF16).astype(F32), jnp.uint32)
    packed = (bits[:, : d // 2] >> 16) | (bits[:, d // 2:] & jnp.uint32(0xFFFF0000))
    nseg = d // 2 // V7X_LANES
    ts = h.shape[0]
    for a in range(nseg):
        hp_ref[pl.ds(a, ts, stride=nseg), :] = packed[:, a * V7X_LANES:(a + 1) * V7X_LANES]
    h_hi = h.astype(BF16)
    h_lo = (h - h_hi.astype(F32)).astype(BF16)
    logits = (jnp.dot(h_hi, rwh_ref[...], preferred_element_type=F32)
              + jnp.dot(h_hi, rwl_ref[...], preferred_element_type=F32)
              + jnp.dot(h_lo, rwh_ref[...], preferred_element_type=F32))
    lane = lax.broadcasted_iota(jnp.int32, logits.shape, 1)
    lane_f = lane.astype(F32)
    lg = jnp.where(lane < n_exp, logits, -jnp.inf)
    v1 = jnp.max(lg, axis=1, keepdims=True)
    i1 = jnp.min(jnp.where(lg == v1, lane_f, float(V7X_LANES)), axis=1, keepdims=True)
    lg2 = jnp.where(lane_f == i1, -jnp.inf, lg)
    v2 = jnp.max(lg2, axis=1, keepdims=True)
    i2 = jnp.min(jnp.where(lg2 == v2, lane_f, float(V7X_LANES)), axis=1, keepdims=True)
    e2 = jnp.exp(v2 - v1)
    g1 = 1.0 / (1.0 + e2)
    g2 = e2 / (1.0 + e2)
    out = jnp.where(lane == 0, i1, 0.0)
    out = jnp.where(lane == 1, i2, out)
    out = jnp.where(lane == 2, g1, out)
    out = jnp.where(lane == 3, g2, out)
    route_ref[...] = out


def _route(x, mod, par, rw_pad, n_exp):
    b, s, d = x.shape
    rw_hi = rw_pad.astype(BF16)
    rw_hi_residual = (rw_pad - rw_hi.astype(F32)).astype(BF16)
    ts = _pick(s, 512, 16)
    nseg = d // 2 // V7X_LANES
    tpb = s // ts
    return pl.pallas_call(
        functools.partial(_route_body, n_exp=n_exp),
        out_shape=(jax.ShapeDtypeStruct((b * s * nseg, V7X_LANES), jnp.uint32),
                   jax.ShapeDtypeStruct((b, s, V7X_LANES), F32)),
        grid=(b, tpb),
        in_specs=[pl.BlockSpec((None, ts, d), lambda bi, i: (bi, i, 0)),
                  pl.BlockSpec((None, 8, d), lambda bi, i: (bi, 0, 0)),
                  pl.BlockSpec((8, d), lambda bi, i: (0, 0)),
                  pl.BlockSpec((d, V7X_LANES), lambda bi, i: (0, 0)),
                  pl.BlockSpec((d, V7X_LANES), lambda bi, i: (0, 0))],
        out_specs=(pl.BlockSpec((ts * nseg, V7X_LANES), lambda bi, i: (bi * tpb + i, 0)),
                   pl.BlockSpec((None, ts, V7X_LANES), lambda bi, i: (bi, i, 0))),
        compiler_params=_params(2),
        name="norm_route",
    )(x, mod, par, rw_hi, rw_hi_residual)


def _weights_changed(te_ref, i):
    return (i == 0) | (te_ref[i] != te_ref[jnp.maximum(i - 1, 0)])


def _up_body(te_ref, nv_ref, a_ref, w1_ref, w3_ref, o_ref, w1b, w3b):
    i = pl.program_id(1)

    @pl.when(_weights_changed(te_ref, i))
    def _():
        w1b[...] = w1_ref[...].astype(BF16)
        w3b[...] = w3_ref[...].astype(BF16)

    @pl.when(i < nv_ref[0])
    def _():
        a = a_ref[...]
        h1 = jnp.dot(a, w1b[...], preferred_element_type=F32)
        h3 = jnp.dot(a, w3b[...], preferred_element_type=F32)
        o_ref[...] = (_silu(h1) * h3).astype(BF16)

    @pl.when(i >= nv_ref[0])
    def _():
        o_ref[...] = jnp.zeros_like(o_ref)


def _swiglu_up(tile_expert, n_valid, a, w1, w3, tm, bn_pref, name):
    m, k = a.shape
    f = w1.shape[2]
    bn = _pick(f, bn_pref, V7X_LANES)
    gs = pltpu.PrefetchScalarGridSpec(
        num_scalar_prefetch=2,
        grid=(f // bn, m // tm),
        in_specs=[pl.BlockSpec((tm, k), lambda j, i, te, nv: (jnp.minimum(i, nv[0] - 1), 0)),
                  pl.BlockSpec((None, k, bn), lambda j, i, te, nv: (te[i], 0, j)),
                  pl.BlockSpec((None, k, bn), lambda j, i, te, nv: (te[i], 0, j))],
        out_specs=pl.BlockSpec((tm, bn), lambda j, i, te, nv: (i, j)),
        scratch_shapes=[pltpu.VMEM((k, bn), BF16), pltpu.VMEM((k, bn), BF16)],
    )
    return pl.pallas_call(
        _up_body,
        out_shape=jax.ShapeDtypeStruct((m, f), BF16),
        grid_spec=gs,
        compiler_params=_params(2),
        name=name,
    )(tile_expert, n_valid, a, w1, w3)


def _down_res_body(a_ref, w_ref, x_ref, mod_ref, o_ref, *, gate_row):
    y = jnp.dot(a_ref[...], w_ref[...].astype(BF16), preferred_element_type=F32)
    o_ref[...] = x_ref[...] + mod_ref[gate_row:gate_row + 1, :] * y


def _matmul_residual(a, w, xres, mod, gate_row, tm_pref, bn_pref, name):
    m, k = a.shape
    n = w.shape[1]
    b = mod.shape[0]
    s = m // b
    tm = _pick(s, tm_pref, 16)
    bn = _pick(n, bn_pref, V7X_LANES)
    return pl.pallas_call(
        functools.partial(_down_res_body, gate_row=gate_row),
        out_shape=jax.ShapeDtypeStruct((m, n), F32),
        grid=(m // tm, n // bn),
        in_specs=[pl.BlockSpec((tm, k), lambda i, j: (i, 0)),
                  pl.BlockSpec((k, bn), lambda i, j: (0, j)),
                  pl.BlockSpec((tm, bn), lambda i, j: (i, j)),
                  pl.BlockSpec((None, 8, bn), lambda i, j: ((i * tm) // s, 0, j))],
        out_specs=pl.BlockSpec((tm, bn), lambda i, j: (i, j)),
        compiler_params=_params(2),
        name=name,
    )(a, w, xres, mod)


def _down_grouped_body(te_ref, nv_ref, a_ref, w_ref, o_ref, wb):
    i = pl.program_id(1)

    @pl.when(_weights_changed(te_ref, i))
    def _():
        wb[...] = w_ref[...].astype(BF16)

    @pl.when(i < nv_ref[0])
    def _():
        o_ref[...] = jnp.dot(a_ref[...], wb[...], preferred_element_type=F32)

    @pl.when(i >= nv_ref[0])
    def _():
        o_ref[...] = jnp.zeros_like(o_ref)


def _down_grouped(tile_expert, n_valid, a, w, tm):
    m, k = a.shape
    n = w.shape[2]
    bn = _pick(n, 1024, V7X_LANES)
    gs = pltpu.PrefetchScalarGridSpec(
        num_scalar_prefetch=2,
        grid=(n // bn, m // tm),
        in_specs=[pl.BlockSpec((tm, k), lambda j, i, te, nv: (jnp.minimum(i, nv[0] - 1), 0)),
                  pl.BlockSpec((None, k, bn), lambda j, i, te, nv: (te[i], 0, j))],
        out_specs=pl.BlockSpec((tm, bn), lambda j, i, te, nv: (i, j)),
        scratch_shapes=[pltpu.VMEM((k, bn), BF16)],
    )
    return pl.pallas_call(
        _down_grouped_body,
        out_shape=jax.ShapeDtypeStruct((m, n), F32),
        grid_spec=gs,
        compiler_params=_params(2),
        name="moe_down",
    )(tile_expert, n_valid, a, w)


def _proj_body(a_ref, w_ref, g_ref, *rest, dils, normed, out_scale, tm, n_row_tiles):
    o_refs, (ybuf, qbuf, wb) = rest[:len(dils)], rest[len(dils):]
    s = pl.program_id(0)
    n_slab = ybuf.shape[0]

    @pl.when(s == 0)
    def _():
        ybuf[...] = jnp.zeros_like(ybuf)

    @pl.when(s % n_row_tiles == 0)
    def _():
        wb[...] = w_ref[...].astype(BF16)

    gvec = g_ref[...] * out_scale

    def head_norm(ys):
        if not normed:
            return ys
        ms = jnp.mean(ys * ys, axis=-1, keepdims=True)
        return (ys * lax.rsqrt(ms + EPS)) * gvec

    out = dict(zip(dils, o_refs))
    quarter = tm // 4
    for sl in range(n_slab):
        if 1 in out:
            out[1][sl, 0] = head_norm(ybuf[sl]).astype(BF16)
        if 4 in out or 16 in out:
            for r4 in range(4):
                ys = head_norm(ybuf[sl, pl.ds(r4, quarter, stride=4), :])
                if 4 in out:
                    out[4][sl, r4] = ys.astype(BF16)
                if 16 in out:
                    qbuf[sl, pl.ds(r4 * quarter, quarter), :] = ys
            if 16 in out:
                for r4 in range(4):
                    for hi in range(4):
                        out[16][sl, r4 + 4 * hi] = qbuf[sl, pl.ds(r4 * quarter + hi, tm // 16, stride=4), :].astype(BF16)

    y = jnp.dot(a_ref[...], wb[...], preferred_element_type=F32)
    for sl in range(n_slab):
        ybuf[sl] = y[:, sl * HEAD_DIM:(sl + 1) * HEAD_DIM]


def _proj_heads(a, w, col0, gvec, dils, normed, out_scale, batch, name):
    m, k = a.shape
    s = m // batch
    d_model = k
    n_heads = d_model // HEAD_DIM
    tm = _pick(s, 1024, 16 * max(dils))
    bn = _pick(d_model, 512, HEAD_DIM)
    hpb = bn // HEAD_DIM
    tpb = s // tm
    ni = m // tm
    n_steps = ni * (d_model // bn)
    cb0 = col0 // bn
    cur = lambda t: jnp.minimum(t, n_steps - 1)
    prev = lambda t: jnp.maximum(t - 1, 0)
    outs = tuple(jax.ShapeDtypeStruct((batch, n_heads, d, s // d, HEAD_DIM), BF16) for d in dils)
    out_specs = tuple(
        pl.BlockSpec((None, hpb, d, tm // d, HEAD_DIM),
                     lambda t: ((prev(t) % ni) // tpb, prev(t) // ni, 0, (prev(t) % ni) % tpb, 0)) for d in dils)
    return pl.pallas_call(
        functools.partial(_proj_body, dils=dils, normed=normed, out_scale=out_scale, tm=tm, n_row_tiles=ni),
        out_shape=outs,
        grid=(n_steps + 1,),
        in_specs=[pl.BlockSpec((tm, k), lambda t: (cur(t) % ni, 0)),
                  pl.BlockSpec((k, bn), lambda t: (0, cb0 + cur(t) // ni)),
                  pl.BlockSpec((1, HEAD_DIM), lambda t: (0, 0))],
        out_specs=out_specs,
        scratch_shapes=[pltpu.VMEM((hpb, tm, HEAD_DIM), F32), pltpu.VMEM((hpb, tm, HEAD_DIM), F32),
                        pltpu.VMEM((k, bn), BF16)],
        compiler_params=_params(1),
        name=name,
    )(a, w, gvec)


def _band_buckets():
    i = np.arange(BLOCK)[:, None]
    kk = np.arange(2 * BLOCK)[None, :]
    rel = i + BLOCK - kk
    valid = (rel >= 0) & (rel <= N_TAPS)
    max_exact = NUM_BUCKETS // 2
    out = []
    for d in DILATIONS:
        n = np.maximum(rel, 0) * d
        nf = np.maximum(n, 1).astype(np.float32)
        large = max_exact + (np.log(nf / np.float32(max_exact)) / np.float32(math.log(MAX_DISTANCE / max_exact))
                             * np.float32(NUM_BUCKETS - max_exact)).astype(np.int32)
        bucket = np.where(n < max_exact, n, np.minimum(large, NUM_BUCKETS - 1))
        out.append(np.where(valid, bucket, -1))
    return np.stack(out).astype(np.int32)


def _bias_body(rb_ref, bk_ref, o_ref, *, n_heads, heads_per_step):
    g = pl.program_id(0)
    h0 = pl.program_id(1) * heads_per_step
    bucket = bk_ref[...]
    for hh in range(heads_per_step):
        acc = jnp.full(bucket.shape, NEG_INF, F32)
        for b in range(NUM_BUCKETS):
            acc = jnp.where(bucket == b, rb_ref[(b * len(DILATIONS) + g) * n_heads + h0 + hh], acc)
        o_ref[hh] = acc


def _bias_bands(rel_bias, n_heads):
    buckets = jnp.asarray(_band_buckets())
    n_g = len(DILATIONS)
    hps = _pick(n_heads, 8, 1)
    return pl.pallas_call(
        functools.partial(_bias_body, n_heads=n_heads, heads_per_step=hps),
        out_shape=jax.ShapeDtypeStruct((n_g, n_heads, BLOCK, 2 * BLOCK), F32),
        grid=(n_g, n_heads // hps),
        in_specs=[pl.BlockSpec(memory_space=pltpu.SMEM),
                  pl.BlockSpec((None, BLOCK, 2 * BLOCK), lambda g, h: (g, 0, 0))],
        out_specs=pl.BlockSpec((None, hps, BLOCK, 2 * BLOCK), lambda g, h: (g, h, 0, 0)),
        compiler_params=_params(2),
        name="bias_bands",
    )(rel_bias.reshape(-1), buckets)


def _attn_body(*refs, hb):
    n_g = len(DILATIONS)
    q_refs = refs[0:n_g]
    k_refs = refs[n_g:3 * n_g:2]
    kh_refs = refs[n_g + 1:3 * n_g:2]
    v_refs = refs[3 * n_g:5 * n_g:2]
    vh_refs = refs[3 * n_g + 1:5 * n_g:2]
    bias_ref = refs[5 * n_g]
    o_ref = refs[5 * n_g + 1]
    scr = refs[5 * n_g + 2:]
    kcats, vcats = scr[0:n_g], scr[n_g:2 * n_g]
    bsc, oscr, mscr, lscr, mrg = scr[2 * n_g:2 * n_g + 5]

    ti = pl.program_id(2)
    col = lax.broadcasted_iota(jnp.int32, (BLOCK, 2 * BLOCK), 1)
    pen = jnp.where((col < BLOCK) & (ti == 0), NEG_INF, 0.0)

    for h in range(hb):
        for g in range(n_g):
            band = bias_ref[g, h]
            bsc[0, g] = band
            bsc[1, g] = band + pen

        for g, d in enumerate(DILATIONS):
            n = ATTN_TILE // d
            nj = n // BLOCK
            kcat, vcat = kcats[g], vcats[g]
            kcat[:, 0:BLOCK, :] = kh_refs[g][h]
            kcat[:, BLOCK:, :] = k_refs[g][h]
            vcat[:, 0:BLOCK, :] = vh_refs[g][h]
            vcat[:, BLOCK:, :] = v_refs[g][h]
            q_ref = q_refs[g]

            def unit(u, carry, g=g, d=d, nj=nj, kcat=kcat, vcat=vcat, q_ref=q_ref, h=h):
                r = u // nj
                j = u % nj
                row = pl.multiple_of(j * BLOCK, BLOCK)
                q = q_ref[h, r, pl.ds(row, BLOCK), :]
                kk = kcat[r, pl.ds(row, 2 * BLOCK), :]
                vv = vcat[r, pl.ds(row, 2 * BLOCK), :]
                s = lax.dot_general(q, kk, (((1,), (1,)), ((), ())), preferred_element_type=F32)
                s = s + bsc[jnp.where(j == 0, 1, 0), g]
                m = jnp.max(s, axis=1, keepdims=True)
                p = jnp.exp(s - m)
                l = jnp.sum(p, axis=1, keepdims=True)
                acc = jnp.dot(p.astype(BF16), vv, preferred_element_type=F32)
                quarter = ATTN_TILE // 4
                if d == 1:
                    rows = pl.ds(pl.multiple_of(j * BLOCK, BLOCK), BLOCK)
                elif d == 4:
                    rows = pl.ds(pl.multiple_of(r * quarter + j * BLOCK, BLOCK), BLOCK)
                else:
                    rows = pl.ds((r % 4) * quarter + r // 4, BLOCK, stride=4)
                oscr[g, rows, :] = acc
                mscr[g, rows, :] = jnp.broadcast_to(m, (BLOCK, HEAD_DIM))
                lscr[g, rows, :] = jnp.broadcast_to(l, (BLOCK, HEAD_DIM))
                return carry

            lax.fori_loop(0, d * nj, unit, 0, unroll=ATTN_UNROLL)

        chunk = 2 * BLOCK
        quarter = ATTN_TILE // 4

        def merge(idx, carry):
            r4 = idx // (quarter // chunk)
            c = idx % (quarter // chunk)
            tok = pl.ds(r4 + 4 * c * chunk, chunk, stride=4)
            sub = pl.ds(pl.multiple_of(r4 * quarter + c * chunk, chunk), chunk)
            rows = [tok if d == 1 else sub for d in DILATIONS]
            ms = [mscr[g, rows[g], :] for g in range(n_g)]
            mx = functools.reduce(jnp.maximum, ms)
            es = [jnp.exp(m - mx) for m in ms]
            num = sum(e * oscr[g, rows[g], :] for g, e in enumerate(es))
            den = sum(e * lscr[g, rows[g], :] for g, e in enumerate(es))
            mrg[tok, :] = num * (1.0 / den)
            return carry

        lax.fori_loop(0, ATTN_TILE // chunk, merge, 0)
        o_ref[:, h * HEAD_DIM:(h + 1) * HEAD_DIM] = mrg[...].astype(BF16)


def _attention(qs, ks, vs, bias, batch, seq, n_heads):
    n_g = len(DILATIONS)
    hb = ATTN_HEADS if n_heads % ATTN_HEADS == 0 else 1
    in_specs, args = [], []
    for g, d in enumerate(DILATIONS):
        n = ATTN_TILE // d
        in_specs.append(pl.BlockSpec((None, hb, d, n, HEAD_DIM), lambda b, h, t: (b, h, 0, t, 0)))
        args.append(qs[g])
    for src in (ks, vs):
        for g, d in enumerate(DILATIONS):
            n = ATTN_TILE // d
            nb = n // BLOCK
            in_specs.append(pl.BlockSpec((None, hb, d, n, HEAD_DIM), lambda b, h, t: (b, h, 0, t, 0)))
            in_specs.append(pl.BlockSpec((None, hb, d, BLOCK, HEAD_DIM),
                                         lambda b, h, t, nb=nb: (b, h, 0, jnp.maximum(t * nb - 1, 0), 0)))
            args += [src[g], src[g]]
    in_specs.append(pl.BlockSpec((n_g, hb, BLOCK, 2 * BLOCK), lambda b, h, t: (0, h, 0, 0)))
    args.append(bias)
    scratch = [pltpu.VMEM((d, BLOCK + ATTN_TILE // d, HEAD_DIM), BF16) for d in DILATIONS] * 2
    scratch += [pltpu.VMEM((2, n_g, BLOCK, 2 * BLOCK), F32)]
    scratch += [pltpu.VMEM((n_g, ATTN_TILE, HEAD_DIM), F32)] * 3
    scratch += [pltpu.VMEM((ATTN_TILE, HEAD_DIM), F32)]
    return pl.pallas_call(
        functools.partial(_attn_body, hb=hb),
        out_shape=jax.ShapeDtypeStruct((batch, seq, n_heads * HEAD_DIM), BF16),
        grid=(batch, n_heads // hb, seq // ATTN_TILE),
        in_specs=in_specs,
        out_specs=pl.BlockSpec((None, ATTN_TILE, hb * HEAD_DIM), lambda b, h, t: (b, t, h)),
        scratch_shapes=scratch,
        compiler_params=_params(3),
        name="dilated_attn",
    )(*args)


def _gather_body(nrows_ref, idx_ref, idxn_ref, src_ref, o_ref, buf, sem, *, rb, nseg):
    i = pl.program_id(0)
    n = pl.num_programs(0)
    holds_rows = lambda blk: blk * rb < nrows_ref[0]

    def start_block(ids_ref, slot):
        def issue(r, c):
            pltpu.make_async_copy(src_ref.at[pl.ds(ids_ref[0, r] * nseg, nseg)],
                                  buf.at[slot, pl.ds(r * nseg, nseg)], sem.at[slot]).start()
            return c
        lax.fori_loop(0, rb, issue, 0, unroll=8)

    @pl.when((i == 0) & holds_rows(0))
    def _():
        start_block(idx_ref, 0)

    @pl.when((i + 1 < n) & holds_rows(i + 1))
    def _():
        start_block(idxn_ref, (i + 1) % 2)

    half = nseg * V7X_LANES
    for slot in range(2):
        @pl.when(holds_rows(i) & (i % 2 == slot))
        def _(slot=slot):
            pltpu.make_async_copy(src_ref.at[pl.ds(0, rb * nseg)], buf.at[slot], sem.at[slot]).wait()
            for a in range(nseg):
                u = buf[slot, pl.ds(a, rb, stride=nseg), :]
                o_ref[:, a * V7X_LANES:(a + 1) * V7X_LANES] = pltpu.bitcast(u << 16, F32).astype(BF16)
                o_ref[:, half + a * V7X_LANES:half + (a + 1) * V7X_LANES] = (
                    pltpu.bitcast(u & jnp.uint32(0xFFFF0000), F32).astype(BF16))

    @pl.when(jnp.logical_not(holds_rows(i)))
    def _():
        o_ref[...] = jnp.zeros_like(o_ref)


def _gather_rows(n_rows_valid, src_idx, hp, d, rb):
    n_rows = src_idx.shape[0]
    nseg = d // 2 // V7X_LANES
    nblk = n_rows // rb
    idx3 = src_idx.reshape(nblk, 1, rb)
    gs = pltpu.PrefetchScalarGridSpec(
        num_scalar_prefetch=1,
        grid=(nblk,),
        in_specs=[pl.BlockSpec((None, 1, rb), lambda i, nr: (i, 0, 0), memory_space=pltpu.SMEM),
                  pl.BlockSpec((None, 1, rb), lambda i, nr: (jnp.minimum(i + 1, nblk - 1), 0, 0),
                               memory_space=pltpu.SMEM),
                  pl.BlockSpec(memory_space=pl.ANY)],
        out_specs=pl.BlockSpec((rb, d), lambda i, nr: (i, 0)),
        scratch_shapes=[pltpu.VMEM((2, rb * nseg, V7X_LANES), jnp.uint32), pltpu.SemaphoreType.DMA((2,))],
    )
    return pl.pallas_call(
        functools.partial(_gather_body, rb=rb, nseg=nseg),
        out_shape=jax.ShapeDtypeStruct((n_rows, d), BF16),
        grid_spec=gs,
        compiler_params=_params(1),
        name="moe_gather",
    )(n_rows_valid, idx3, idx3, hp)


def _combine_body(pos_ref, posn_ref, y_ref, x_ref, route_ref, mod_ref, o_ref, buf, sem, *, tc, gate_row):
    i = pl.program_id(0)
    n = pl.num_programs(0)

    def start_block(p_ref, slot):
        def issue(r, c):
            pltpu.make_async_copy(y_ref.at[pl.ds(p_ref[0, r], 1)], buf.at[slot, 0, pl.ds(r, 1)], sem.at[slot]).start()
            pltpu.make_async_copy(y_ref.at[pl.ds(p_ref[0, tc + r], 1)], buf.at[slot, 1, pl.ds(r, 1)],
                                  sem.at[slot]).start()
            return c
        lax.fori_loop(0, tc, issue, 0, unroll=8)

    @pl.when(i == 0)
    def _():
        start_block(pos_ref, 0)

    @pl.when(i + 1 < n)
    def _():
        start_block(posn_ref, (i + 1) % 2)

    g1 = route_ref[:, 2:3]
    g2 = route_ref[:, 3:4]
    for slot in range(2):
        @pl.when(i % 2 == slot)
        def _(slot=slot):
            pltpu.make_async_copy(y_ref.at[pl.ds(0, tc)], buf.at[slot, 0], sem.at[slot]).wait()
            pltpu.make_async_copy(y_ref.at[pl.ds(0, tc)], buf.at[slot, 1], sem.at[slot]).wait()
            y = g1 * buf[slot, 0] + g2 * buf[slot, 1]
            o_ref[...] = x_ref[...] + mod_ref[gate_row:gate_row + 1, :] * y


def _combine(pos, y, x, route, mod, gate_row):
    m, d = x.shape
    b = mod.shape[0]
    s = m // b
    tc = pos.shape[2] // 2
    nblk = m // tc
    return pl.pallas_call(
        functools.partial(_combine_body, tc=tc, gate_row=gate_row),
        out_shape=jax.ShapeDtypeStruct((m, d), F32),
        grid=(nblk,),
        in_specs=[pl.BlockSpec((None, 1, 2 * tc), lambda i: (i, 0, 0), memory_space=pltpu.SMEM),
                  pl.BlockSpec((None, 1, 2 * tc), lambda i: (jnp.minimum(i + 1, nblk - 1), 0, 0),
                               memory_space=pltpu.SMEM),
                  pl.BlockSpec(memory_space=pl.ANY),
                  pl.BlockSpec((tc, d), lambda i: (i, 0)),
                  pl.BlockSpec((tc, V7X_LANES), lambda i: (i, 0)),
                  pl.BlockSpec((None, 8, d), lambda i: ((i * tc) // s, 0, 0))],
        out_specs=pl.BlockSpec((tc, d), lambda i: (i, 0)),
        scratch_shapes=[pltpu.VMEM((2, 2, tc, d), F32), pltpu.SemaphoreType.DMA((2,))],
        compiler_params=_params(1),
        name="moe_combine",
    )(pos, pos, y, x, route, mod)


def _rows8(*rows):
    b, d = rows[0].shape
    pad = [jnp.zeros((b, d), F32)] * (8 - len(rows))
    return jnp.stack(list(rows) + pad, axis=1)


def _par8(*rows):
    d = rows[0].shape[0]
    pad = [jnp.zeros((d,), F32)] * (8 - len(rows))
    return jnp.stack(list(rows) + pad, axis=0)


def kernel(x, c, ada_w, ada_b, norm_g, pool_w, pool_scale, kv_ada_w, kv_ada_b, kv_norm_g, w_k, w_v, k_norm_g,
           w_q, q_norm_g, w_o, rel_bias, ffn_w1, ffn_w3, ffn_w2, router_w, moe_w1, moe_w3, moe_w2):
    batch, seq, d = x.shape
    n_tok = batch * seq
    n_heads = d // HEAD_DIM
    n_exp = router_w.shape[2]
    assert d % HEAD_DIM == 0 and seq % ATTN_TILE == 0 and batch <= 8

    c8 = jnp.pad(c, ((0, 8 - batch), (0, 0)))
    mods = _ada(c8, ada_w.reshape(4, d, 3 * d), ada_b.reshape(4, 1, 3 * d))[:, :batch]
    kvm = _ada(c8, kv_ada_w[None], kv_ada_b[None, None])[0, :batch]
    sh = lambda m: (mods[m, :, :d], mods[m, :, d:2 * d], mods[m, :, 2 * d:])
    (s00, c00, g00), (s01, c01, g01), (s10, c10, g10), (s11, c11, g11) = sh(0), sh(1), sh(2), sh(3)
    kv_shift, kv_scale = kvm[:, :d], kvm[:, d:]

    x1, h2 = _mixer(x, _rows8(s00, c00, g00, s01, c01), _par8(norm_g[0, 0], norm_g[0, 1], pool_scale[0]),
                    pool_w[0].astype(BF16))
    tm_ffn = _pick(n_tok, 2048, 16)
    gact = _swiglu_up(jnp.zeros((n_tok // tm_ffn,), jnp.int32), jnp.full((1,), n_tok // tm_ffn, jnp.int32),
                      h2.reshape(n_tok, d), ffn_w1[:1], ffn_w3[:1], tm_ffn, 256, "ffn_up")
    x2 = _matmul_residual(gact, ffn_w2[0].astype(BF16), x1.reshape(n_tok, d), _rows8(g01), 0, 512, 512, "ffn_down")

    kvh, h3 = _norm2(x2.reshape(batch, seq, d), _rows8(kv_shift, kv_scale, s10, c10), _par8(kv_norm_g, norm_g[1, 0]))
    kvh, h3 = kvh.reshape(n_tok, d), h3.reshape(n_tok, d)
    ks = _proj_heads(kvh, w_k, 0, k_norm_g.reshape(1, HEAD_DIM), DILATIONS, True, 1.0, batch, "proj_k")
    vs = _proj_heads(kvh, w_v, 0, k_norm_g.reshape(1, HEAD_DIM), DILATIONS, False, 1.0, batch, "proj_v")
    qs = [_proj_heads(h3, w_q[0], g * d, q_norm_g[0].reshape(1, HEAD_DIM), (dl,), True, HEAD_DIM ** -0.5, batch,
                      f"proj_q{dl}")[0] for g, dl in enumerate(DILATIONS)]

    bias = _bias_bands(rel_bias, n_heads)
    att = _attention(qs, ks, vs, bias, batch, seq, n_heads)
    x3 = _matmul_residual(att.reshape(n_tok, d), w_o[0].astype(BF16), x2, _rows8(g10), 0, 1024, 512, "attn_out")

    rw_pad = jnp.pad(router_w[0], ((0, 0), (0, V7X_LANES - n_exp)))
    hp, route = _route(x3.reshape(batch, seq, d), _rows8(s11, c11), _par8(norm_g[1, 1]), rw_pad, n_exp)
    route = route.reshape(n_tok, V7X_LANES)

    tm = _pick(n_tok, 512, 256)
    r_pad = TOP_K * n_tok + n_exp * tm
    e_flat = jnp.concatenate([route[:, 0], route[:, 1]]).astype(jnp.int32)
    onehot = (e_flat[:, None] == jnp.arange(n_exp, dtype=jnp.int32)[None, :]).astype(jnp.int32)
    csum = jnp.cumsum(onehot, axis=0)
    rank = jnp.sum((csum - 1) * onehot, axis=1)
    padded = ((csum[-1] + tm - 1) // tm) * tm
    gend = jnp.cumsum(padded)
    dest = (gend - padded)[e_flat] + rank
    tok = jnp.arange(TOP_K * n_tok, dtype=jnp.int32) % n_tok
    src_idx = jnp.zeros((r_pad,), jnp.int32).at[dest].set(tok)
    tile_expert = jnp.minimum(
        jnp.searchsorted(gend, jnp.arange(r_pad // tm, dtype=jnp.int32) * tm, side="right"), n_exp - 1
    ).astype(jnp.int32)

    n_valid = (gend[-1:] // tm).astype(jnp.int32)

    xg = _gather_rows(gend[-1:].astype(jnp.int32), src_idx, hp, d, tm)
    hg = _swiglu_up(tile_expert, n_valid, xg, moe_w1[0], moe_w3[0], tm, 512, "moe_up")
    yg = _down_grouped(tile_expert, n_valid, hg, moe_w2[0], tm)

    tc = _pick(n_tok, 256, 8)
    pos = jnp.concatenate([dest[:n_tok].reshape(n_tok // tc, 1, tc), dest[n_tok:].reshape(n_tok // tc, 1, tc)], axis=2)
    out = _combine(pos.astype(jnp.int32), yg, x3, route, _rows8(g11), 0)
    return out.reshape(batch, seq, d)
```

```python
import functools
import math

import numpy as np
import jax
import jax.numpy as jnp
from jax import lax
from jax.experimental import pallas as pl
from jax.experimental.pallas import tpu as pltpu

F32 = jnp.float32
BF16 = jnp.bfloat16

V7X_LANES = 128
V7X_VMEM_LIMIT_BYTES = 60 * 1024 * 1024

HEAD_DIM = 128
BLOCK = 128
DILATIONS = (1, 4, 16)
N_TAPS = 128
NUM_BUCKETS = 32
MAX_DISTANCE = 2048
POOL_WINDOWS = (2, 4, 8, 16)
POOL_HALO = 16
TOP_K = 2
EPS = 1e-6
NEG_INF = -1e30
ATTN_TILE = BLOCK * DILATIONS[-1]
ATTN_UNROLL = 16
ATTN_HEADS = 2


def _params(n_grid, vmem=V7X_VMEM_LIMIT_BYTES):
    return pltpu.CompilerParams(dimension_semantics=("arbitrary",) * n_grid, vmem_limit_bytes=vmem)


def _pick(n, pref, mult):
    if n <= pref:
        return n
    t = (pref // mult) * mult
    while t >= mult:
        if n % t == 0:
            return t
        t -= mult
    return n


def _silu(v):
    return v * jax.nn.sigmoid(v)


def _norm_mod(v, g, scale, shift):
    ms = jnp.mean(v * v, axis=-1, keepdims=True)
    return (v * lax.rsqrt(ms + EPS)) * (g * (1.0 + scale)) + shift


def _ada_body(c_ref, w_ref, b_ref, o_ref):
    cond = _silu(c_ref[...]).astype(BF16)
    o_ref[...] = jnp.dot(cond, w_ref[...].astype(BF16), preferred_element_type=F32) + b_ref[...]


def _ada(c8, w, b):
    m, d, n = w.shape
    bn = _pick(n, 1024, V7X_LANES)
    return pl.pallas_call(
        _ada_body,
        out_shape=jax.ShapeDtypeStruct((m, 8, n), F32),
        grid=(m, n // bn),
        in_specs=[
            pl.BlockSpec((8, d), lambda i, j: (0, 0)),
            pl.BlockSpec((None, d, bn), lambda i, j: (i, 0, j)),
            pl.BlockSpec((None, 1, bn), lambda i, j: (i, 0, j)),
        ],
        out_specs=pl.BlockSpec((None, 8, bn), lambda i, j: (i, 0, j)),
        compiler_params=_params(2),
        name="ada",
    )(c8, w, b)


def _mixer_body(x_ref, halo_ref, mod_ref, par_ref, pw_ref, x1_ref, h2_ref, *, ts, group):
    i = pl.program_id(1)
    x = x_ref[...]
    g_a, g_f, pscale = par_ref[0:1, :], par_ref[1:2, :], par_ref[2:3, :]
    shift_a, scale_a, gate_a = mod_ref[0:1, :], mod_ref[1:2, :], mod_ref[2:3, :]
    shift_f, scale_f = mod_ref[3:4, :], mod_ref[4:5, :]

    h = _norm_mod(x, g_a, scale_a, shift_a)
    h_halo = jnp.where(i > 0, _norm_mod(halo_ref[...], g_a, scale_a, shift_a), 0.0)
    hh = jnp.concatenate([h_halo, h], axis=0)

    lt = min(V7X_LANES, group)
    pos = i * ts + lax.broadcasted_iota(jnp.int32, (ts, lt), 0)
    ys = []
    for g, k in enumerate(POOL_WINDOWS):
        sl = slice(g * group, (g + 1) * group)
        a = hh[:, sl]
        span = 1
        while span < k:
            a = a[:-span] + a[span:]
            span *= 2
        start = POOL_HALO + 1 - k
        win = a[start:start + ts]
        inv_cnt = 1.0 / jnp.minimum(pos + 1, k).astype(F32)
        hg = h[:, sl]
        p = jnp.concatenate([win[:, c:c + lt] * inv_cnt - hg[:, c:c + lt] for c in range(0, group, lt)], axis=1)
        ys.append(jnp.dot(p.astype(BF16), pw_ref[g], preferred_element_type=F32))
    x1 = x + jnp.concatenate(ys, axis=1) * (gate_a * pscale)
    x1_ref[...] = x1
    h2_ref[...] = _norm_mod(x1, g_f, scale_f, shift_f).astype(BF16)


def _mixer(x, mod, par, pool_w):
    b, s, d = x.shape
    group = d // len(POOL_WINDOWS)
    ts = _pick(s, 256, POOL_HALO)
    hpb = ts // POOL_HALO
    return pl.pallas_call(
        functools.partial(_mixer_body, ts=ts, group=group),
        out_shape=(jax.ShapeDtypeStruct((b, s, d), F32), jax.ShapeDtypeStruct((b, s, d), BF16)),
        grid=(b, s // ts),
        in_specs=[
            pl.BlockSpec((None, ts, d), lambda bi, i: (bi, i, 0)),
            pl.BlockSpec((None, POOL_HALO, d), lambda bi, i: (bi, jnp.maximum(i * hpb - 1, 0), 0)),
            pl.BlockSpec((None, 8, d), lambda bi, i: (bi, 0, 0)),
            pl.BlockSpec((8, d), lambda bi, i: (0, 0)),
            pl.BlockSpec((len(POOL_WINDOWS), group, group), lambda bi, i: (0, 0, 0),
                         pipeline_mode=pl.Buffered(1)),
        ],
        out_specs=(pl.BlockSpec((None, ts, d), lambda bi, i: (bi, i, 0)),
                   pl.BlockSpec((None, ts, d), lambda bi, i: (bi, i, 0))),
        compiler_params=_params(2),
        name="mixer0",
    )(x, x, mod, par, pool_w)


def _norm2_body(x_ref, mod_ref, par_ref, a_ref, b_ref):
    x = x_ref[...]
    a_ref[...] = _norm_mod(x, par_ref[0:1, :], mod_ref[1:2, :], mod_ref[0:1, :]).astype(BF16)
    b_ref[...] = _norm_mod(x, par_ref[1:2, :], mod_ref[3:4, :], mod_ref[2:3, :]).astype(BF16)


def _norm2(x, mod, par):
    b, s, d = x.shape
    ts = _pick(s, 512, 16)
    row = pl.BlockSpec((None, ts, d), lambda bi, i: (bi, i, 0))
    return pl.pallas_call(
        _norm2_body,
        out_shape=(jax.ShapeDtypeStruct((b, s, d), BF16), jax.ShapeDtypeStruct((b, s, d), BF16)),
        grid=(b, s // ts),
        in_specs=[row, pl.BlockSpec((None, 8, d), lambda bi, i: (bi, 0, 0)),
                  pl.BlockSpec((8, d), lambda bi, i: (0, 0))],
        out_specs=(row, row),
        compiler_params=_params(2),
        name="norm_kv_q",
    )(x, mod, par)


def _route_body(x_ref, mod_ref, par_ref, rwh_ref, rwl_ref, hp_ref, route_ref, *, n_exp):
    x = x_ref[...]
    h = _norm_mod(x, par_ref[0:1, :], mod_ref[1:2, :], mod_ref[0:1, :])
    d = h.shape[1]
    bits = pltpu.bitcast(h.astype(BF16).astype(F32), jnp.uint32)
    packed = (bits[:, : d // 2] >> 16) | (bits[:, d // 2:] & jnp.uint32(0xFFFF0000))
    nseg = d // 2 // V7X_LANES
    ts = h.shape[0]
    for a in range(nseg):
        hp_ref[pl.ds(a, ts, stride=nseg), :] = packed[:, a * V7X_LANES:(a + 1) * V7X_LANES]
    h_hi = h.astype(BF16)
    h_lo = (h - h_hi.astype(F32)).astype(BF16)
    logits = (jnp.dot(h_hi, rwh_ref[...], preferred_element_type=F32)
              + jnp.dot(h_hi, rwl_ref[...], preferred_element_type=F32)
              + jnp.dot(h_lo, rwh_ref[...], preferred_element_type=F32))
    lane = lax.broadcasted_iota(jnp.int32, logits.shape, 1)
    lane_f = lane.astype(F32)
    lg = jnp.where(lane < n_exp, logits, -jnp.inf)
    v1 = jnp.max(lg, axis=1, keepdims=True)
    i1 = jnp.min(jnp.where(lg == v1, lane_f, float(V7X_LANES)), axis=1, keepdims=True)
    lg2 = jnp.where(lane_f == i1, -jnp.inf, lg)
    v2 = jnp.max(lg2, axis=1, keepdims=True)
    i2 = jnp.min(jnp.where(lg2 == v2, lane_f, float(V7X_LANES)), axis=1, keepdims=True)
    e2 = jnp.exp(v2 - v1)
    g1 = 1.0 / (1.0 + e2)
    g2 = e2 / (1.0 + e2)
    out = jnp.where(lane == 0, i1, 0.0)
    out = jnp.where(lane == 1, i2, out)
    out = jnp.where(lane == 2, g1, out)
    out = jnp.where(lane == 3, g2, out)
    route_ref[...] = out


def _route(x, mod, par, rw_pad, n_exp):
    b, s, d = x.shape
    rw_hi = rw_pad.astype(BF16)
    rw_hi_residual = (rw_pad - rw_hi.astype(F32)).astype(BF16)
    ts = _pick(s, 512, 16)
    nseg = d // 2 // V7X_LANES
    tpb = s // ts
    return pl.pallas_call(
        functools.partial(_route_body, n_exp=n_exp),
        out_shape=(jax.ShapeDtypeStruct((b * s * nseg, V7X_LANES), jnp.uint32),
                   jax.ShapeDtypeStruct((b, s, V7X_LANES), F32)),
        grid=(b, tpb),
        in_specs=[pl.BlockSpec((None, ts, d), lambda bi, i: (bi, i, 0)),
                  pl.BlockSpec((None, 8, d), lambda bi, i: (bi, 0, 0)),
                  pl.BlockSpec((8, d), lambda bi, i: (0, 0)),
                  pl.BlockSpec((d, V7X_LANES), lambda bi, i: (0, 0)),
                  pl.BlockSpec((d, V7X_LANES), lambda bi, i: (0, 0))],
        out_specs=(pl.BlockSpec((ts * nseg, V7X_LANES), lambda bi, i: (bi * tpb + i, 0)),
                   pl.BlockSpec((None, ts, V7X_LANES), lambda bi, i: (bi, i, 0))),
        compiler_params=_params(2),
        name="norm_route",
    )(x, mod, par, rw_hi, rw_hi_residual)


def _weights_changed(te_ref, i):
    return (i == 0) | (te_ref[i] != te_ref[jnp.maximum(i - 1, 0)])


def _up_body(te_ref, nv_ref, a_ref, w1_ref, w3_ref, o_ref, w1b, w3b):
    i = pl.program_id(1)

    @pl.when(_weights_changed(te_ref, i))
    def _():
        w1b[...] = w1_ref[...].astype(BF16)
        w3b[...] = w3_ref[...].astype(BF16)

    @pl.when(i < nv_ref[0])
    def _():
        a = a_ref[...]
        h1 = jnp.dot(a, w1b[...], preferred_element_type=F32)
        h3 = jnp.dot(a, w3b[...], preferred_element_type=F32)
        o_ref[...] = (_silu(h1) * h3).astype(BF16)

    @pl.when(i >= nv_ref[0])
    def _():
        o_ref[...] = jnp.zeros_like(o_ref)


def _swiglu_up(tile_expert, n_valid, a, w1, w3, tm, bn_pref, name):
    m, k = a.shape
    f = w1.shape[2]
    bn = _pick(f, bn_pref, V7X_LANES)
    gs = pltpu.PrefetchScalarGridSpec(
        num_scalar_prefetch=2,
        grid=(f // bn, m // tm),
        in_specs=[pl.BlockSpec((tm, k), lambda j, i, te, nv: (jnp.minimum(i, nv[0] - 1), 0)),
                  pl.BlockSpec((None, k, bn), lambda j, i, te, nv: (te[i], 0, j)),
                  pl.BlockSpec((None, k, bn), lambda j, i, te, nv: (te[i], 0, j))],
        out_specs=pl.BlockSpec((tm, bn), lambda j, i, te, nv: (i, j)),
        scratch_shapes=[pltpu.VMEM((k, bn), BF16), pltpu.VMEM((k, bn), BF16)],
    )
    return pl.pallas_call(
        _up_body,
        out_shape=jax.ShapeDtypeStruct((m, f), BF16),
        grid_spec=gs,
        compiler_params=_params(2),
        name=name,
    )(tile_expert, n_valid, a, w1, w3)


def _down_res_body(a_ref, w_ref, x_ref, mod_ref, o_ref, *, gate_row):
    y = jnp.dot(a_ref[...], w_ref[...].astype(BF16), preferred_element_type=F32)
    o_ref[...] = x_ref[...] + mod_ref[gate_row:gate_row + 1, :] * y


def _matmul_residual(a, w, xres, mod, gate_row, tm_pref, bn_pref, name):
    m, k = a.shape
    n = w.shape[1]
    b = mod.shape[0]
    s = m // b
    tm = _pick(s, tm_pref, 16)
    bn = _pick(n, bn_pref, V7X_LANES)
    return pl.pallas_call(
        functools.partial(_down_res_body, gate_row=gate_row),
        out_shape=jax.ShapeDtypeStruct((m, n), F32),
        grid=(m // tm, n // bn),
        in_specs=[pl.BlockSpec((tm, k), lambda i, j: (i, 0)),
                  pl.BlockSpec((k, bn), lambda i, j: (0, j)),
                  pl.BlockSpec((tm, bn), lambda i, j: (i, j)),
                  pl.BlockSpec((None, 8, bn), lambda i, j: ((i * tm) // s, 0, j))],
        out_specs=pl.BlockSpec((tm, bn), lambda i, j: (i, j)),
        compiler_params=_params(2),
        name=name,
    )(a, w, xres, mod)


def _down_grouped_body(te_ref, nv_ref, a_ref, w_ref, o_ref, wb):
    i = pl.program_id(1)

    @pl.when(_weights_changed(te_ref, i))
    def _():
        wb[...] = w_ref[...].astype(BF16)

    @pl.when(i < nv_ref[0])
    def _():
        o_ref[...] = jnp.dot(a_ref[...], wb[...], preferred_element_type=F32)

    @pl.when(i >= nv_ref[0])
    def _():
        o_ref[...] = jnp.zeros_like(o_ref)


def _down_grouped(tile_expert, n_valid, a, w, tm):
    m, k = a.shape
    n = w.shape[2]
    bn = _pick(n, 1024, V7X_LANES)
    gs = pltpu.PrefetchScalarGridSpec(
        num_scalar_prefetch=2,
        grid=(n // bn, m // tm),
        in_specs=[pl.BlockSpec((tm, k), lambda j, i, te, nv: (jnp.minimum(i, nv[0] - 1), 0)),
                  pl.BlockSpec((None, k, bn), lambda j, i, te, nv: (te[i], 0, j))],
        out_specs=pl.BlockSpec((tm, bn), lambda j, i, te, nv: (i, j)),
        scratch_shapes=[pltpu.VMEM((k, bn), BF16)],
    )
    return pl.pallas_call(
        _down_grouped_body,
        out_shape=jax.ShapeDtypeStruct((m, n), F32),
        grid_spec=gs,
        compiler_params=_params(2),
        name="moe_down",
    )(tile_expert, n_valid, a, w)


def _proj_body(a_ref, w_ref, g_ref, *rest, dils, normed, out_scale, tm, n_row_tiles):
    o_refs, (ybuf, qbuf, wb) = rest[:len(dils)], rest[len(dils):]
    s = pl.program_id(0)
    n_slab = ybuf.shape[0]

    @pl.when(s == 0)
    def _():
        ybuf[...] = jnp.zeros_like(ybuf)

    @pl.when(s % n_row_tiles == 0)
    def _():
        wb[...] = w_ref[...].astype(BF16)

    gvec = g_ref[...] * out_scale

    def head_norm(ys):
        if not normed:
            return ys
        ms = jnp.mean(ys * ys, axis=-1, keepdims=True)
        return (ys * lax.rsqrt(ms + EPS)) * gvec

    out = dict(zip(dils, o_refs))
    quarter = tm // 4
    for sl in range(n_slab):
        if 1 in out:
            out[1][sl, 0] = head_norm(ybuf[sl]).astype(BF16)
        if 4 in out or 16 in out:
            for r4 in range(4):
                ys = head_norm(ybuf[sl, pl.ds(r4, quarter, stride=4), :])
                if 4 in out:
                    out[4][sl, r4] = ys.astype(BF16)
                if 16 in out:
                    qbuf[sl, pl.ds(r4 * quarter, quarter), :] = ys
            if 16 in out:
                for r4 in range(4):
                    for hi in range(4):
                        out[16][sl, r4 + 4 * hi] = qbuf[sl, pl.ds(r4 * quarter + hi, tm // 16, stride=4), :].astype(BF16)

    y = jnp.dot(a_ref[...], wb[...], preferred_element_type=F32)
    for sl in range(n_slab):
        ybuf[sl] = y[:, sl * HEAD_DIM:(sl + 1) * HEAD_DIM]


def _proj_heads(a, w, col0, gvec, dils, normed, out_scale, batch, name):
    m, k = a.shape
    s = m // batch
    d_model = k
    n_heads = d_model // HEAD_DIM
    tm = _pick(s, 1024, 16 * max(dils))
    bn = _pick(d_model, 512, HEAD_DIM)
    hpb = bn // HEAD_DIM
    tpb = s // tm
    ni = m // tm
    n_steps = ni * (d_model // bn)
    cb0 = col0 // bn
    cur = lambda t: jnp.minimum(t, n_steps - 1)
    prev = lambda t: jnp.maximum(t - 1, 0)
    outs = tuple(jax.ShapeDtypeStruct((batch, n_heads, d, s // d, HEAD_DIM), BF16) for d in dils)
    out_specs = tuple(
        pl.BlockSpec((None, hpb, d, tm // d, HEAD_DIM),
                     lambda t: ((prev(t) % ni) // tpb, prev(t) // ni, 0, (prev(t) % ni) % tpb, 0)) for d in dils)
    return pl.pallas_call(
        functools.partial(_proj_body, dils=dils, normed=normed, out_scale=out_scale, tm=tm, n_row_tiles=ni),
        out_shape=outs,
        grid=(n_steps + 1,),
        in_specs=[pl.BlockSpec((tm, k), lambda t: (cur(t) % ni, 0)),
                  pl.BlockSpec((k, bn), lambda t: (0, cb0 + cur(t) // ni)),
                  pl.BlockSpec((1, HEAD_DIM), lambda t: (0, 0))],
        out_specs=out_specs,
        scratch_shapes=[pltpu.VMEM((hpb, tm, HEAD_DIM), F32), pltpu.VMEM((hpb, tm, HEAD_DIM), F32),
                        pltpu.VMEM((k, bn), BF16)],
        compiler_params=_params(1),
        name=name,
    )(a, w, gvec)


def _band_buckets():
    i = np.arange(BLOCK)[:, None]
    kk = np.arange(2 * BLOCK)[None, :]
    rel = i + BLOCK - kk
    valid = (rel >= 0) & (rel <= N_TAPS)
    max_exact = NUM_BUCKETS // 2
    out = []
    for d in DILATIONS:
        n = np.maximum(rel, 0) * d
        nf = np.maximum(n, 1).astype(np.float32)
        large = max_exact + (np.log(nf / np.float32(max_exact)) / np.float32(math.log(MAX_DISTANCE / max_exact))
                             * np.float32(NUM_BUCKETS - max_exact)).astype(np.int32)
        bucket = np.where(n < max_exact, n, np.minimum(large, NUM_BUCKETS - 1))
        out.append(np.where(valid, bucket, -1))
    return np.stack(out).astype(np.int32)


def _bias_body(rb_ref, bk_ref, o_ref, *, n_heads, heads_per_step):
    g = pl.program_id(0)
    h0 = pl.program_id(1) * heads_per_step
    bucket = bk_ref[...]
    for hh in range(heads_per_step):
        acc = jnp.full(bucket.shape, NEG_INF, F32)
        for b in range(NUM_BUCKETS):
            acc = jnp.where(bucket == b, rb_ref[(b * len(DILATIONS) + g) * n_heads + h0 + hh], acc)
        o_ref[hh] = acc


def _bias_bands(rel_bias, n_heads):
    buckets = jnp.asarray(_band_buckets())
    n_g = len(DILATIONS)
    hps = _pick(n_heads, 8, 1)
    return pl.pallas_call(
        functools.partial(_bias_body, n_heads=n_heads, heads_per_step=hps),
        out_shape=jax.ShapeDtypeStruct((n_g, n_heads, BLOCK, 2 * BLOCK), F32),
        grid=(n_g, n_heads // hps),
        in_specs=[pl.BlockSpec(memory_space=pltpu.SMEM),
                  pl.BlockSpec((None, BLOCK, 2 * BLOCK), lambda g, h: (g, 0, 0))],
        out_specs=pl.BlockSpec((None, hps, BLOCK, 2 * BLOCK), lambda g, h: (g, h, 0, 0)),
        compiler_params=_params(2),
        name="bias_bands",
    )(rel_bias.reshape(-1), buckets)


def _attn_body(*refs, hb):
    n_g = len(DILATIONS)
    q_refs = refs[0:n_g]
    k_refs = refs[n_g:3 * n_g:2]
    kh_refs = refs[n_g + 1:3 * n_g:2]
    v_refs = refs[3 * n_g:5 * n_g:2]
    vh_refs = refs[3 * n_g + 1:5 * n_g:2]
    bias_ref = refs[5 * n_g]
    o_ref = refs[5 * n_g + 1]
    scr = refs[5 * n_g + 2:]
    kcats, vcats = scr[0:n_g], scr[n_g:2 * n_g]
    bsc, oscr, mscr, lscr, mrg = scr[2 * n_g:2 * n_g + 5]

    ti = pl.program_id(2)
    col = lax.broadcasted_iota(jnp.int32, (BLOCK, 2 * BLOCK), 1)
    pen = jnp.where((col < BLOCK) & (ti == 0), NEG_INF, 0.0)

    for h in range(hb):
        for g in range(n_g):
            band = bias_ref[g, h]
            bsc[0, g] = band
            bsc[1, g] = band + pen

        for g, d in enumerate(DILATIONS):
            n = ATTN_TILE // d
            nj = n // BLOCK
            kcat, vcat = kcats[g], vcats[g]
            kcat[:, 0:BLOCK, :] = kh_refs[g][h]
            kcat[:, BLOCK:, :] = k_refs[g][h]
            vcat[:, 0:BLOCK, :] = vh_refs[g][h]
            vcat[:, BLOCK:, :] = v_refs[g][h]
            q_ref = q_refs[g]

            def unit(u, carry, g=g, d=d, nj=nj, kcat=kcat, vcat=vcat, q_ref=q_ref, h=h):
                r = u // nj
                j = u % nj
                row = pl.multiple_of(j * BLOCK, BLOCK)
                q = q_ref[h, r, pl.ds(row, BLOCK), :]
                kk = kcat[r, pl.ds(row, 2 * BLOCK), :]
                vv = vcat[r, pl.ds(row, 2 * BLOCK), :]
                s = lax.dot_general(q, kk, (((1,), (1,)), ((), ())), preferred_element_type=F32)
                s = s + bsc[jnp.where(j == 0, 1, 0), g]
                m = jnp.max(s, axis=1, keepdims=True)
                p = jnp.exp(s - m)
                l = jnp.sum(p, axis=1, keepdims=True)
                acc = jnp.dot(p.astype(BF16), vv, preferred_element_type=F32)
                quarter = ATTN_TILE // 4
                if d == 1:
                    rows = pl.ds(pl.multiple_of(j * BLOCK, BLOCK), BLOCK)
                elif d == 4:
                    rows = pl.ds(pl.multiple_of(r * quarter + j * BLOCK, BLOCK), BLOCK)
                else:
                    rows = pl.ds((r % 4) * quarter + r // 4, BLOCK, stride=4)
                oscr[g, rows, :] = acc
                mscr[g, rows, :] = jnp.broadcast_to(m, (BLOCK, HEAD_DIM))
                lscr[g, rows, :] = jnp.broadcast_to(l, (BLOCK, HEAD_DIM))
                return carry

            lax.fori_loop(0, d * nj, unit, 0, unroll=ATTN_UNROLL)

        chunk = 2 * BLOCK
        quarter = ATTN_TILE // 4

        def merge(idx, carry):
            r4 = idx // (quarter // chunk)
            c = idx % (quarter // chunk)
            tok = pl.ds(r4 + 4 * c * chunk, chunk, stride=4)
            sub = pl.ds(pl.multiple_of(r4 * quarter + c * chunk, chunk), chunk)
            rows = [tok if d == 1 else sub for d in DILATIONS]
            ms = [mscr[g, rows[g], :] for g in range(n_g)]
            mx = functools.reduce(jnp.maximum, ms)
            es = [jnp.exp(m - mx) for m in ms]
            num = sum(e * oscr[g, rows[g], :] for g, e in enumerate(es))
            den = sum(e * lscr[g, rows[g], :] for g, e in enumerate(es))
            mrg[tok, :] = num * (1.0 / den)
            return carry

        lax.fori_loop(0, ATTN_TILE // chunk, merge, 0)
        o_ref[:, h * HEAD_DIM:(h + 1) * HEAD_DIM] = mrg[...].astype(BF16)


def _attention(qs, ks, vs, bias, batch, seq, n_heads):
    n_g = len(DILATIONS)
    hb = ATTN_HEADS if n_heads % ATTN_HEADS == 0 else 1
    in_specs, args = [], []
    for g, d in enumerate(DILATIONS):
        n = ATTN_TILE // d
        in_specs.append(pl.BlockSpec((None, hb, d, n, HEAD_DIM), lambda b, h, t: (b, h, 0, t, 0)))
        args.append(qs[g])
    for src in (ks, vs):
        for g, d in enumerate(DILATIONS):
            n = ATTN_TILE // d
            nb = n // BLOCK
            in_specs.append(pl.BlockSpec((None, hb, d, n, HEAD_DIM), lambda b, h, t: (b, h, 0, t, 0)))
            in_specs.append(pl.BlockSpec((None, hb, d, BLOCK, HEAD_DIM),
                                         lambda b, h, t, nb=nb: (b, h, 0, jnp.maximum(t * nb - 1, 0), 0)))
            args += [src[g], src[g]]
    in_specs.append(pl.BlockSpec((n_g, hb, BLOCK, 2 * BLOCK), lambda b, h, t: (0, h, 0, 0)))
    args.append(bias)
    scratch = [pltpu.VMEM((d, BLOCK + ATTN_TILE // d, HEAD_DIM), BF16) for d in DILATIONS] * 2
    scratch += [pltpu.VMEM((2, n_g, BLOCK, 2 * BLOCK), F32)]
    scratch += [pltpu.VMEM((n_g, ATTN_TILE, HEAD_DIM), F32)] * 3
    scratch += [pltpu.VMEM((ATTN_TILE, HEAD_DIM), F32)]
    return pl.pallas_call(
        functools.partial(_attn_body, hb=hb),
        out_shape=jax.ShapeDtypeStruct((batch, seq, n_heads * HEAD_DIM), BF16),
        grid=(batch, n_heads // hb, seq // ATTN_TILE),
        in_specs=in_specs,
        out_specs=pl.BlockSpec((None, ATTN_TILE, hb * HEAD_DIM), lambda b, h, t: (b, t, h)),
        scratch_shapes=scratch,
        compiler_params=_params(3),
        name="dilated_attn",
    )(*args)


def _gather_body(nrows_ref, idx_ref, idxn_ref, src_ref, o_ref, buf, sem, *, rb, nseg):
    i = pl.program_id(0)
    n = pl.num_programs(0)
    holds_rows = lambda blk: blk * rb < nrows_ref[0]

    def start_block(ids_ref, slot):
        def issue(r, c):
            pltpu.make_async_copy(src_ref.at[pl.ds(ids_ref[0, r] * nseg, nseg)],
                                  buf.at[slot, pl.ds(r * nseg, nseg)], sem.at[slot]).start()
            return c
        lax.fori_loop(0, rb, issue, 0, unroll=8)

    @pl.when((i == 0) & holds_rows(0))
    def _():
        start_block(idx_ref, 0)

    @pl.when((i + 1 < n) & holds_rows(i + 1))
    def _():
        start_block(idxn_ref, (i + 1) % 2)

    half = nseg * V7X_LANES
    for slot in range(2):
        @pl.when(holds_rows(i) & (i % 2 == slot))
        def _(slot=slot):
            pltpu.make_async_copy(src_ref.at[pl.ds(0, rb * nseg)], buf.at[slot], sem.at[slot]).wait()
            for a in range(nseg):
                u = buf[slot, pl.ds(a, rb, stride=nseg), :]
                o_ref[:, a * V7X_LANES:(a + 1) * V7X_LANES] = pltpu.bitcast(u << 16, F32).astype(BF16)
                o_ref[:, half + a * V7X_LANES:half + (a + 1) * V7X_LANES] = (
                    pltpu.bitcast(u & jnp.uint32(0xFFFF0000), F32).astype(BF16))

    @pl.when(jnp.logical_not(holds_rows(i)))
    def _():
        o_ref[...] = jnp.zeros_like(o_ref)


def _gather_rows(n_rows_valid, src_idx, hp, d, rb):
    n_rows = src_idx.shape[0]
    nseg = d // 2 // V7X_LANES
    nblk = n_rows // rb
    idx3 = src_idx.reshape(nblk, 1, rb)
    gs = pltpu.PrefetchScalarGridSpec(
        num_scalar_prefetch=1,
        grid=(nblk,),
        in_specs=[pl.BlockSpec((None, 1, rb), lambda i, nr: (i, 0, 0), memory_space=pltpu.SMEM),
                  pl.BlockSpec((None, 1, rb), lambda i, nr: (jnp.minimum(i + 1, nblk - 1), 0, 0),
                               memory_space=pltpu.SMEM),
                  pl.BlockSpec(memory_space=pl.ANY)],
        out_specs=pl.BlockSpec((rb, d), lambda i, nr: (i, 0)),
        scratch_shapes=[pltpu.VMEM((2, rb * nseg, V7X_LANES), jnp.uint32), pltpu.SemaphoreType.DMA((2,))],
    )
    return pl.pallas_call(
        functools.partial(_gather_body, rb=rb, nseg=nseg),
        out_shape=jax.ShapeDtypeStruct((n_rows, d), BF16),
        grid_spec=gs,
        compiler_params=_params(1),
        name="moe_gather",
    )(n_rows_valid, idx3, idx3, hp)


def _combine_body(pos_ref, posn_ref, y_ref, x_ref, route_ref, mod_ref, o_ref, buf, sem, *, tc, gate_row):
    i = pl.program_id(0)
    n = pl.num_programs(0)

    def start_block(p_ref, slot):
        def issue(r, c):
            pltpu.make_async_copy(y_ref.at[pl.ds(p_ref[0, r], 1)], buf.at[slot, 0, pl.ds(r, 1)], sem.at[slot]).start()
            pltpu.make_async_copy(y_ref.at[pl.ds(p_ref[0, tc + r], 1)], buf.at[slot, 1, pl.ds(r, 1)],
                                  sem.at[slot]).start()
            return c
        lax.fori_loop(0, tc, issue, 0, unroll=8)

    @pl.when(i == 0)
    def _():
        start_block(pos_ref, 0)

    @pl.when(i + 1 < n)
    def _():
        start_block(posn_ref, (i + 1) % 2)

    g1 = route_ref[:, 2:3]
    g2 = route_ref[:, 3:4]
    for slot in range(2):
        @pl.when(i % 2 == slot)
        def _(slot=slot):
            pltpu.make_async_copy(y_ref.at[pl.ds(0, tc)], buf.at[slot, 0], sem.at[slot]).wait()
            pltpu.make_async_copy(y_ref.at[pl.ds(0, tc)], buf.at[slot, 1], sem.at[slot]).wait()
            y = g1 * buf[slot, 0] + g2 * buf[slot, 1]
            o_ref[...] = x_ref[...] + mod_ref[gate_row:gate_row + 1, :] * y


def _combine(pos, y, x, route, mod, gate_row):
    m, d = x.shape
    b = mod.shape[0]
    s = m // b
    tc = pos.shape[2] // 2
    nblk = m // tc
    return pl.pallas_call(
        functools.partial(_combine_body, tc=tc, gate_row=gate_row),
        out_shape=jax.ShapeDtypeStruct((m, d), F32),
        grid=(nblk,),
        in_specs=[pl.BlockSpec((None, 1, 2 * tc), lambda i: (i, 0, 0), memory_space=pltpu.SMEM),
                  pl.BlockSpec((None, 1, 2 * tc), lambda i: (jnp.minimum(i + 1, nblk - 1), 0, 0),
                               memory_space=pltpu.SMEM),
                  pl.BlockSpec(memory_space=pl.ANY),
                  pl.BlockSpec((tc, d), lambda i: (i, 0)),
                  pl.BlockSpec((tc, V7X_LANES), lambda i: (i, 0)),
                  pl.BlockSpec((None, 8, d), lambda i: ((i * tc) // s, 0, 0))],
        out_specs=pl.BlockSpec((tc, d), lambda i: (i, 0)),
        scratch_shapes=[pltpu.VMEM((2, 2, tc, d), F32), pltpu.SemaphoreType.DMA((2,))],
        compiler_params=_params(1),
        name="moe_combine",
    )(pos, pos, y, x, route, mod)


def _rows8(*rows):
    b, d = rows[0].shape
    pad = [jnp.zeros((b, d), F32)] * (8 - len(rows))
    return jnp.stack(list(rows) + pad, axis=1)


def _par8(*rows):
    d = rows[0].shape[0]
    pad = [jnp.zeros((d,), F32)] * (8 - len(rows))
    return jnp.stack(list(rows) + pad, axis=0)


def kernel(x, c, ada_w, ada_b, norm_g, pool_w, pool_scale, kv_ada_w, kv_ada_b, kv_norm_g, w_k, w_v, k_norm_g,
           w_q, q_norm_g, w_o, rel_bias, ffn_w1, ffn_w3, ffn_w2, router_w, moe_w1, moe_w3, moe_w2):
    batch, seq, d = x.shape
    n_tok = batch * seq
    n_heads = d // HEAD_DIM
    n_exp = router_w.shape[2]
    assert d % HEAD_DIM == 0 and seq % ATTN_TILE == 0 and batch <= 8

    c8 = jnp.pad(c, ((0, 8 - batch), (0, 0)))
    mods = _ada(c8, ada_w.reshape(4, d, 3 * d), ada_b.reshape(4, 1, 3 * d))[:, :batch]
    kvm = _ada(c8, kv_ada_w[None], kv_ada_b[None, None])[0, :batch]
    sh = lambda m: (mods[m, :, :d], mods[m, :, d:2 * d], mods[m, :, 2 * d:])
    (s00, c00, g00), (s01, c01, g01), (s10, c10, g10), (s11, c11, g11) = sh(0), sh(1), sh(2), sh(3)
    kv_shift, kv_scale = kvm[:, :d], kvm[:, d:]

    x1, h2 = _mixer(x, _rows8(s00, c00, g00, s01, c01), _par8(norm_g[0, 0], norm_g[0, 1], pool_scale[0]),
                    pool_w[0].astype(BF16))
    tm_ffn = _pick(n_tok, 2048, 16)
    gact = _swiglu_up(jnp.zeros((n_tok // tm_ffn,), jnp.int32), jnp.full((1,), n_tok // tm_ffn, jnp.int32),
                      h2.reshape(n_tok, d), ffn_w1[:1], ffn_w3[:1], tm_ffn, 256, "ffn_up")
    x2 = _matmul_residual(gact, ffn_w2[0].astype(BF16), x1.reshape(n_tok, d), _rows8(g01), 0, 512, 512, "ffn_down")

    kvh, h3 = _norm2(x2.reshape(batch, seq, d), _rows8(kv_shift, kv_scale, s10, c10), _par8(kv_norm_g, norm_g[1, 0]))
    kvh, h3 = kvh.reshape(n_tok, d), h3.reshape(n_tok, d)
    ks = _proj_heads(kvh, w_k, 0, k_norm_g.reshape(1, HEAD_DIM), DILATIONS, True, 1.0, batch, "proj_k")
    vs = _proj_heads(kvh, w_v, 0, k_norm_g.reshape(1, HEAD_DIM), DILATIONS, False, 1.0, batch, "proj_v")
    qs = [_proj_heads(h3, w_q[0], g * d, q_norm_g[0].reshape(1, HEAD_DIM), (dl,), True, HEAD_DIM ** -0.5, batch,
                      f"proj_q{dl}")[0] for g, dl in enumerate(DILATIONS)]

    bias = _bias_bands(rel_bias, n_heads)
    att = _attention(qs, ks, vs, bias, batch, seq, n_heads)
    x3 = _matmul_residual(att.reshape(n_tok, d), w_o[0].astype(BF16), x2, _rows8(g10), 0, 1024, 1024, "attn_out")

    rw_pad = jnp.pad(router_w[0], ((0, 0), (0, V7X_LANES - n_exp)))
    hp, route = _route(x3.reshape(batch, seq, d), _rows8(s11, c11), _par8(norm_g[1, 1]), rw_pad, n_exp)
    route = route.reshape(n_tok, V7X_LANES)

    tm = _pick(n_tok, 512, 256)
    r_pad = TOP_K * n_tok + n_exp * tm
    e_flat = jnp.concatenate([route[:, 0], route[:, 1]]).astype(jnp.int32)
    onehot = (e_flat[:, None] == jnp.arange(n_exp, dtype=jnp.int32)[None, :]).astype(jnp.int32)
    csum = jnp.cumsum(onehot, axis=0)
    rank = jnp.sum((csum - 1) * onehot, axis=1)
    padded = ((csum[-1] + tm - 1) // tm) * tm
    gend = jnp.cumsum(padded)
    dest = (gend - padded)[e_flat] + rank
    tok = jnp.arange(TOP_K * n_tok, dtype=jnp.int32) % n_tok
    src_idx = jnp.zeros((r_pad,), jnp.int32).at[dest].set(tok)
    tile_expert = jnp.minimum(
        jnp.searchsorted(gend, jnp.arange(r_pad // tm, dtype=jnp.int32) * tm, side="right"), n_exp - 1
    ).astype(jnp.int32)

    n_valid = (gend[-1:] // tm).astype(jnp.int32)

    xg = _gather_rows(gend[-1:].astype(jnp.int32), src_idx, hp, d, tm)
    hg = _swiglu_up(tile_expert, n_valid, xg, moe_w1[0], moe_w3[0], tm, 512, "moe_up")
    yg = _down_grouped(tile_expert, n_valid, hg, moe_w2[0], tm)

    tc = _pick(n_tok, 256, 8)
    pos = jnp.concatenate([dest[:n_tok].reshape(n_tok // tc, 1, tc), dest[n_tok:].reshape(n_tok // tc, 1, tc)], axis=2)
    out = _combine(pos.astype(jnp.int32), yg, x3, route, _rows8(g11), 0)
    return out.reshape(batch, seq, d)
```

```python
import functools
import math

import numpy as np
import jax
import jax.numpy as jnp
from jax import lax
from jax.experimental import pallas as pl
from jax.experimental.pallas import tpu as pltpu

F32 = jnp.float32
BF16 = jnp.bfloat16

V7X_LANES = 128
V7X_VMEM_LIMIT_BYTES = 60 * 1024 * 1024

HEAD_DIM = 128
BLOCK = 128
DILATIONS = (1, 4, 16)
N_TAPS = 128
NUM_BUCKETS = 32
MAX_DISTANCE = 2048
POOL_WINDOWS = (2, 4, 8, 16)
POOL_HALO = 16
TOP_K = 2
EPS = 1e-6
NEG_INF = -1e30
ATTN_TILE = BLOCK * DILATIONS[-1]
ATTN_UNROLL = 16
ATTN_HEADS = 2


def _params(n_grid, vmem=V7X_VMEM_LIMIT_BYTES):
    return pltpu.CompilerParams(dimension_semantics=("arbitrary",) * n_grid, vmem_limit_bytes=vmem)


def _pick(n, pref, mult):
    if n <= pref:
        return n
    t = (pref // mult) * mult
    while t >= mult:
        if n % t == 0:
            return t
        t -= mult
    return n


def _silu(v):
    return v * jax.nn.sigmoid(v)


def _norm_mod(v, g, scale, shift):
    ms = jnp.mean(v * v, axis=-1, keepdims=True)
    return (v * lax.rsqrt(ms + EPS)) * (g * (1.0 + scale)) + shift


def _ada_body(c_ref, w_ref, b_ref, o_ref):
    cond = _silu(c_ref[...]).astype(BF16)
    o_ref[...] = jnp.dot(cond, w_ref[...].astype(BF16), preferred_element_type=F32) + b_ref[...]


def _ada(c8, w, b):
    m, d, n = w.shape
    bn = _pick(n, 1024, V7X_LANES)
    return pl.pallas_call(
        _ada_body,
        out_shape=jax.ShapeDtypeStruct((m, 8, n), F32),
        grid=(m, n // bn),
        in_specs=[
            pl.BlockSpec((8, d), lambda i, j: (0, 0)),
            pl.BlockSpec((None, d, bn), lambda i, j: (i, 0, j)),
            pl.BlockSpec((None, 1, bn), lambda i, j: (i, 0, j)),
        ],
        out_specs=pl.BlockSpec((None, 8, bn), lambda i, j: (i, 0, j)),
        compiler_params=_params(2),
        name="ada",
    )(c8, w, b)


def _mixer_body(x_ref, halo_ref, mod_ref, par_ref, pw_ref, x1_ref, h2_ref, *, ts, group):
    i = pl.program_id(1)
    x = x_ref[...]
    g_a, g_f, pscale = par_ref[0:1, :], par_ref[1:2, :], par_ref[2:3, :]
    shift_a, scale_a, gate_a = mod_ref[0:1, :], mod_ref[1:2, :], mod_ref[2:3, :]
    shift_f, scale_f = mod_ref[3:4, :], mod_ref[4:5, :]

    h = _norm_mod(x, g_a, scale_a, shift_a)
    h_halo = jnp.where(i > 0, _norm_mod(halo_ref[...], g_a, scale_a, shift_a), 0.0)
    hh = jnp.concatenate([h_halo, h], axis=0)

    lt = min(V7X_LANES, group)
    pos = i * ts + lax.broadcasted_iota(jnp.int32, (ts, lt), 0)
    ys = []
    for g, k in enumerate(POOL_WINDOWS):
        sl = slice(g * group, (g + 1) * group)
        a = hh[:, sl]
        span = 1
        while span < k:
            a = a[:-span] + a[span:]
            span *= 2
        start = POOL_HALO + 1 - k
        win = a[start:start + ts]
        inv_cnt = 1.0 / jnp.minimum(pos + 1, k).astype(F32)
        hg = h[:, sl]
        p = jnp.concatenate([win[:, c:c + lt] * inv_cnt - hg[:, c:c + lt] for c in range(0, group, lt)], axis=1)
        ys.append(jnp.dot(p.astype(BF16), pw_ref[g], preferred_element_type=F32))
    x1 = x + jnp.concatenate(ys, axis=1) * (gate_a * pscale)
    x1_ref[...] = x1
    h2_ref[...] = _norm_mod(x1, g_f, scale_f, shift_f).astype(BF16)


def _mixer(x, mod, par, pool_w):
    b, s, d = x.shape
    group = d // len(POOL_WINDOWS)
    ts = _pick(s, 256, POOL_HALO)
    hpb = ts // POOL_HALO
    return pl.pallas_call(
        functools.partial(_mixer_body, ts=ts, group=group),
        out_shape=(jax.ShapeDtypeStruct((b, s, d), F32), jax.ShapeDtypeStruct((b, s, d), BF16)),
        grid=(b, s // ts),
        in_specs=[
            pl.BlockSpec((None, ts, d), lambda bi, i: (bi, i, 0)),
            pl.BlockSpec((None, POOL_HALO, d), lambda bi, i: (bi, jnp.maximum(i * hpb - 1, 0), 0)),
            pl.BlockSpec((None, 8, d), lambda bi, i: (bi, 0, 0)),
            pl.BlockSpec((8, d), lambda bi, i: (0, 0)),
            pl.BlockSpec((len(POOL_WINDOWS), group, group), lambda bi, i: (0, 0, 0),
                         pipeline_mode=pl.Buffered(1)),
        ],
        out_specs=(pl.BlockSpec((None, ts, d), lambda bi, i: (bi, i, 0)),
                   pl.BlockSpec((None, ts, d), lambda bi, i: (bi, i, 0))),
        compiler_params=_params(2),
        name="mixer0",
    )(x, x, mod, par, pool_w)


def _norm2_body(x_ref, mod_ref, par_ref, a_ref, b_ref):
    x = x_ref[...]
    a_ref[...] = _norm_mod(x, par_ref[0:1, :], mod_ref[1:2, :], mod_ref[0:1, :]).astype(BF16)
    b_ref[...] = _norm_mod(x, par_ref[1:2, :], mod_ref[3:4, :], mod_ref[2:3, :]).astype(BF16)


def _norm2(x, mod, par):
    b, s, d = x.shape
    ts = _pick(s, 512, 16)
    row = pl.BlockSpec((None, ts, d), lambda bi, i: (bi, i, 0))
    return pl.pallas_call(
        _norm2_body,
        out_shape=(jax.ShapeDtypeStruct((b, s, d), BF16), jax.ShapeDtypeStruct((b, s, d), BF16)),
        grid=(b, s // ts),
        in_specs=[row, pl.BlockSpec((None, 8, d), lambda bi, i: (bi, 0, 0)),
                  pl.BlockSpec((8, d), lambda bi, i: (0, 0))],
        out_specs=(row, row),
        compiler_params=_params(2),
        name="norm_kv_q",
    )(x, mod, par)


def _route_body(x_ref, mod_ref, par_ref, rwh_ref, rwl_ref, hp_ref, route_ref, *, n_exp):
    x = x_ref[...]
    h = _norm_mod(x, par_ref[0:1, :], mod_ref[1:2, :], mod_ref[0:1, :])
    d = h.shape[1]
    bits = pltpu.bitcast(h.astype(BF16).astype(F32), jnp.uint32)
    packed = (bits[:, : d // 2] >> 16) | (bits[:, d // 2:] & jnp.uint32(0xFFFF0000))
    nseg = d // 2 // V7X_LANES
    ts = h.shape[0]
    for a in range(nseg):
        hp_ref[pl.ds(a, ts, stride=nseg), :] = packed[:, a * V7X_LANES:(a + 1) * V7X_LANES]
    h_hi = h.astype(BF16)
    h_lo = (h - h_hi.astype(F32)).astype(BF16)
    logits = (jnp.dot(h_hi, rwh_ref[...], preferred_element_type=F32)
              + jnp.dot(h_hi, rwl_ref[...], preferred_element_type=F32)
              + jnp.dot(h_lo, rwh_ref[...], preferred_element_type=F32))
    lane = lax.broadcasted_iota(jnp.int32, logits.shape, 1)
    lane_f = lane.astype(F32)
    lg = jnp.where(lane < n_exp, logits, -jnp.inf)
    v1 = jnp.max(lg, axis=1, keepdims=True)
    i1 = jnp.min(jnp.where(lg == v1, lane_f, float(V7X_LANES)), axis=1, keepdims=True)
    lg2 = jnp.where(lane_f == i1, -jnp.inf, lg)
    v2 = jnp.max(lg2, axis=1, keepdims=True)
    i2 = jnp.min(jnp.where(lg2 == v2, lane_f, float(V7X_LANES)), axis=1, keepdims=True)
    e2 = jnp.exp(v2 - v1)
    g1 = 1.0 / (1.0 + e2)
    g2 = e2 / (1.0 + e2)
    out = jnp.where(lane == 0, i1, 0.0)
    out = jnp.where(lane == 1, i2, out)
    out = jnp.where(lane == 2, g1, out)
    out = jnp.where(lane == 3, g2, out)
    route_ref[...] = out


def _route(x, mod, par, rw_pad, n_exp):
    b, s, d = x.shape
    rw_hi = rw_pad.astype(BF16)
    rw_hi_residual = (rw_pad - rw_hi.astype(F32)).astype(BF16)
    ts = _pick(s, 512, 16)
    nseg = d // 2 // V7X_LANES
    tpb = s // ts
    return pl.pallas_call(
        functools.partial(_route_body, n_exp=n_exp),
        out_shape=(jax.ShapeDtypeStruct((b * s * nseg, V7X_LANES), jnp.uint32),
                   jax.ShapeDtypeStruct((b, s, V7X_LANES), F32)),
        grid=(b, tpb),
        in_specs=[pl.BlockSpec((None, ts, d), lambda bi, i: (bi, i, 0)),
                  pl.BlockSpec((None, 8, d), lambda bi, i: (bi, 0, 0)),
                  pl.BlockSpec((8, d), lambda bi, i: (0, 0)),
                  pl.BlockSpec((d, V7X_LANES), lambda bi, i: (0, 0)),
                  pl.BlockSpec((d, V7X_LANES), lambda bi, i: (0, 0))],
        out_specs=(pl.BlockSpec((ts * nseg, V7X_LANES), lambda bi, i: (bi * tpb + i, 0)),
                   pl.BlockSpec((None, ts, V7X_LANES), lambda bi, i: (bi, i, 0))),
        compiler_params=_params(2),
        name="norm_route",
    )(x, mod, par, rw_hi, rw_hi_residual)


def _weights_changed(te_ref, i):
    return (i == 0) | (te_ref[i] != te_ref[jnp.maximum(i - 1, 0)])


def _up_body(te_ref, nv_ref, a_ref, w1_ref, w3_ref, o_ref, w1b, w3b):
    i = pl.program_id(1)

    @pl.when(_weights_changed(te_ref, i))
    def _():
        w1b[...] = w1_ref[...].astype(BF16)
        w3b[...] = w3_ref[...].astype(BF16)

    @pl.when(i < nv_ref[0])
    def _():
        a = a_ref[...]
        h1 = jnp.dot(a, w1b[...], preferred_element_type=F32)
        h3 = jnp.dot(a, w3b[...], preferred_element_type=F32)
        o_ref[...] = (_silu(h1) * h3).astype(BF16)

    @pl.when(i >= nv_ref[0])
    def _():
        o_ref[...] = jnp.zeros_like(o_ref)


def _swiglu_up(tile_expert, n_valid, a, w1, w3, tm, bn_pref, name):
    m, k = a.shape
    f = w1.shape[2]
    bn = _pick(f, bn_pref, V7X_LANES)
    gs = pltpu.PrefetchScalarGridSpec(
        num_scalar_prefetch=2,
        grid=(f // bn, m // tm),
        in_specs=[pl.BlockSpec((tm, k), lambda j, i, te, nv: (jnp.minimum(i, nv[0] - 1), 0)),
                  pl.BlockSpec((None, k, bn), lambda j, i, te, nv: (te[i], 0, j)),
                  pl.BlockSpec((None, k, bn), lambda j, i, te, nv: (te[i], 0, j))],
        out_specs=pl.BlockSpec((tm, bn), lambda j, i, te, nv: (i, j)),
        scratch_shapes=[pltpu.VMEM((k, bn), BF16), pltpu.VMEM((k, bn), BF16)],
    )
    return pl.pallas_call(
        _up_body,
        out_shape=jax.ShapeDtypeStruct((m, f), BF16),
        grid_spec=gs,
        compiler_params=_params(2),
        name=name,
    )(tile_expert, n_valid, a, w1, w3)


def _down_res_body(a_ref, w_ref, x_ref, mod_ref, o_ref, *, gate_row):
    y = jnp.dot(a_ref[...], w_ref[...].astype(BF16), preferred_element_type=F32)
    o_ref[...] = x_ref[...] + mod_ref[gate_row:gate_row + 1, :] * y


def _matmul_residual(a, w, xres, mod, gate_row, tm_pref, bn_pref, name):
    m, k = a.shape
    n = w.shape[1]
    b = mod.shape[0]
    s = m // b
    tm = _pick(s, tm_pref, 16)
    bn = _pick(n, bn_pref, V7X_LANES)
    return pl.pallas_call(
        functools.partial(_down_res_body, gate_row=gate_row),
        out_shape=jax.ShapeDtypeStruct((m, n), F32),
        grid=(m // tm, n // bn),
        in_specs=[pl.BlockSpec((tm, k), lambda i, j: (i, 0)),
                  pl.BlockSpec((k, bn), lambda i, j: (0, j)),
                  pl.BlockSpec((tm, bn), lambda i, j: (i, j)),
                  pl.BlockSpec((None, 8, bn), lambda i, j: ((i * tm) // s, 0, j))],
        out_specs=pl.BlockSpec((tm, bn), lambda i, j: (i, j)),
        compiler_params=_params(2),
        name=name,
    )(a, w, xres, mod)


def _down_grouped_body(te_ref, nv_ref, a_ref, w_ref, o_ref, wb):
    i = pl.program_id(1)

    @pl.when(_weights_changed(te_ref, i))
    def _():
        wb[...] = w_ref[...].astype(BF16)

    @pl.when(i < nv_ref[0])
    def _():
        o_ref[...] = jnp.dot(a_ref[...], wb[...], preferred_element_type=F32)

    @pl.when(i >= nv_ref[0])
    def _():
        o_ref[...] = jnp.zeros_like(o_ref)


def _down_grouped(tile_expert, n_valid, a, w, tm):
    m, k = a.shape
    n = w.shape[2]
    bn = _pick(n, 1024, V7X_LANES)
    gs = pltpu.PrefetchScalarGridSpec(
        num_scalar_prefetch=2,
        grid=(n // bn, m // tm),
        in_specs=[pl.BlockSpec((tm, k), lambda j, i, te, nv: (jnp.minimum(i, nv[0] - 1), 0)),
                  pl.BlockSpec((None, k, bn), lambda j, i, te, nv: (te[i], 0, j))],
        out_specs=pl.BlockSpec((tm, bn), lambda j, i, te, nv: (i, j)),
        scratch_shapes=[pltpu.VMEM((k, bn), BF16)],
    )
    return pl.pallas_call(
        _down_grouped_body,
        out_shape=jax.ShapeDtypeStruct((m, n), F32),
        grid_spec=gs,
        compiler_params=_params(2),
        name="moe_down",
    )(tile_expert, n_valid, a, w)


def _proj_body(a_ref, w_ref, g_ref, *rest, dils, normed, out_scale, tm, n_row_tiles):
    o_refs, (ybuf, qbuf, wb) = rest[:len(dils)], rest[len(dils):]
    s = pl.program_id(0)
    n_slab = ybuf.shape[0]

    @pl.when(s == 0)
    def _():
        ybuf[...] = jnp.zeros_like(ybuf)

    @pl.when(s % n_row_tiles == 0)
    def _():
        wb[...] = w_ref[...].astype(BF16)

    gvec = g_ref[...] * out_scale

    def head_norm(ys):
        if not normed:
            return ys
        ms = jnp.mean(ys * ys, axis=-1, keepdims=True)
        return (ys * lax.rsqrt(ms + EPS)) * gvec

    out = dict(zip(dils, o_refs))
    quarter = tm // 4
    for sl in range(n_slab):
        if 1 in out:
            out[1][sl, 0] = head_norm(ybuf[sl]).astype(BF16)
        if 4 in out or 16 in out:
            for r4 in range(4):
                ys = head_norm(ybuf[sl, pl.ds(r4, quarter, stride=4), :])
                if 4 in out:
                    out[4][sl, r4] = ys.astype(BF16)
                if 16 in out:
                    qbuf[sl, pl.ds(r4 * quarter, quarter), :] = ys
            if 16 in out:
                for r4 in range(4):
                    for hi in range(4):
                        out[16][sl, r4 + 4 * hi] = qbuf[sl, pl.ds(r4 * quarter + hi, tm // 16, stride=4), :].astype(BF16)

    y = jnp.dot(a_ref[...], wb[...], preferred_element_type=F32)
    for sl in range(n_slab):
        ybuf[sl] = y[:, sl * HEAD_DIM:(sl + 1) * HEAD_DIM]


def _proj_heads(a, w, col0, gvec, dils, normed, out_scale, batch, name):
    m, k = a.shape
    s = m // batch
    d_model = k
    n_heads = d_model // HEAD_DIM
    tm = _pick(s, 1024, 16 * max(dils))
    bn = _pick(d_model, 512, HEAD_DIM)
    hpb = bn // HEAD_DIM
    tpb = s // tm
    ni = m // tm
    n_steps = ni * (d_model // bn)
    cb0 = col0 // bn
    cur = lambda t: jnp.minimum(t, n_steps - 1)
    prev = lambda t: jnp.maximum(t - 1, 0)
    outs = tuple(jax.ShapeDtypeStruct((batch, n_heads, d, s // d, HEAD_DIM), BF16) for d in dils)
    out_specs = tuple(
        pl.BlockSpec((None, hpb, d, tm // d, HEAD_DIM),
                     lambda t: ((prev(t) % ni) // tpb, prev(t) // ni, 0, (prev(t) % ni) % tpb, 0)) for d in dils)
    return pl.pallas_call(
        functools.partial(_proj_body, dils=dils, normed=normed, out_scale=out_scale, tm=tm, n_row_tiles=ni),
        out_shape=outs,
        grid=(n_steps + 1,),
        in_specs=[pl.BlockSpec((tm, k), lambda t: (cur(t) % ni, 0)),
                  pl.BlockSpec((k, bn), lambda t: (0, cb0 + cur(t) // ni)),
                  pl.BlockSpec((1, HEAD_DIM), lambda t: (0, 0))],
        out_specs=out_specs,
        scratch_shapes=[pltpu.VMEM((hpb, tm, HEAD_DIM), F32), pltpu.VMEM((hpb, tm, HEAD_DIM), F32),
                        pltpu.VMEM((k, bn), BF16)],
        compiler_params=_params(1),
        name=name,
    )(a, w, gvec)


def _band_buckets():
    i = np.arange(BLOCK)[:, None]
    kk = np.arange(2 * BLOCK)[None, :]
    rel = i + BLOCK - kk
    valid = (rel >= 0) & (rel <= N_TAPS)
    max_exact = NUM_BUCKETS // 2
    out = []
    for d in DILATIONS:
        n = np.maximum(rel, 0) * d
        nf = np.maximum(n, 1).astype(np.float32)
        large = max_exact + (np.log(nf / np.float32(max_exact)) / np.float32(math.log(MAX_DISTANCE / max_exact))
                             * np.float32(NUM_BUCKETS - max_exact)).astype(np.int32)
        bucket = np.where(n < max_exact, n, np.minimum(large, NUM_BUCKETS - 1))
        out.append(np.where(valid, bucket, -1))
    return np.stack(out).astype(np.int32)


def _bias_body(rb_ref, bk_ref, o_ref, *, n_heads, heads_per_step):
    g = pl.program_id(0)
    h0 = pl.program_id(1) * heads_per_step
    bucket = bk_ref[...]
    for hh in range(heads_per_step):
        acc = jnp.full(bucket.shape, NEG_INF, F32)
        for b in range(NUM_BUCKETS):
            acc = jnp.where(bucket == b, rb_ref[(b * len(DILATIONS) + g) * n_heads + h0 + hh], acc)
        o_ref[hh] = acc


def _bias_bands(rel_bias, n_heads):
    buckets = jnp.asarray(_band_buckets())
    n_g = len(DILATIONS)
    hps = _pick(n_heads, 8, 1)
    return pl.pallas_call(
        functools.partial(_bias_body, n_heads=n_heads, heads_per_step=hps),
        out_shape=jax.ShapeDtypeStruct((n_g, n_heads, BLOCK, 2 * BLOCK), F32),
        grid=(n_g, n_heads // hps),
        in_specs=[pl.BlockSpec(memory_space=pltpu.SMEM),
                  pl.BlockSpec((None, BLOCK, 2 * BLOCK), lambda g, h: (g, 0, 0))],
        out_specs=pl.BlockSpec((None, hps, BLOCK, 2 * BLOCK), lambda g, h: (g, h, 0, 0)),
        compiler_params=_params(2),
        name="bias_bands",
    )(rel_bias.reshape(-1), buckets)


def _attn_body(*refs, hb):
    n_g = len(DILATIONS)
    q_refs = refs[0:n_g]
    k_refs = refs[n_g:3 * n_g:2]
    kh_refs = refs[n_g + 1:3 * n_g:2]
    v_refs = refs[3 * n_g:5 * n_g:2]
    vh_refs = refs[3 * n_g + 1:5 * n_g:2]
    bias_ref = refs[5 * n_g]
    o_ref = refs[5 * n_g + 1]
    scr = refs[5 * n_g + 2:]
    kcats, vcats = scr[0:n_g], scr[n_g:2 * n_g]
    bsc, oscr, mscr, lscr, mrg = scr[2 * n_g:2 * n_g + 5]

    ti = pl.program_id(2)
    col = lax.broadcasted_iota(jnp.int32, (BLOCK, 2 * BLOCK), 1)
    pen = jnp.where((col < BLOCK) & (ti == 0), NEG_INF, 0.0)

    @pl.when((pl.program_id(0) == 0) & (pl.program_id(1) == 0) & (ti == 0))
    def _():
        for vcat in vcats:
            vcat[:, :, HEAD_DIM:] = jnp.ones(vcat.shape[:2] + (HEAD_DIM,), BF16)

    for h in range(hb):
        for g in range(n_g):
            band = bias_ref[g, h]
            bsc[0, g] = band
            bsc[1, g] = band + pen

        for g, d in enumerate(DILATIONS):
            n = ATTN_TILE // d
            nj = n // BLOCK
            kcat, vcat = kcats[g], vcats[g]
            kcat[:, 0:BLOCK, :] = kh_refs[g][h]
            kcat[:, BLOCK:, :] = k_refs[g][h]
            vcat[:, 0:BLOCK, 0:HEAD_DIM] = vh_refs[g][h]
            vcat[:, BLOCK:, 0:HEAD_DIM] = v_refs[g][h]
            q_ref = q_refs[g]

            def unit(u, carry, g=g, d=d, nj=nj, kcat=kcat, vcat=vcat, q_ref=q_ref, h=h):
                r = u // nj
                j = u % nj
                row = pl.multiple_of(j * BLOCK, BLOCK)
                q = q_ref[h, r, pl.ds(row, BLOCK), :]
                kk = kcat[r, pl.ds(row, 2 * BLOCK), :]
                vv = vcat[r, pl.ds(row, 2 * BLOCK), :]
                s = lax.dot_general(q, kk, (((1,), (1,)), ((), ())), preferred_element_type=F32)
                s = s + bsc[jnp.where(j == 0, 1, 0), g]
                m = jnp.max(s, axis=1, keepdims=True)
                p = jnp.exp(s - m)
                pv = jnp.dot(p.astype(BF16), vv, preferred_element_type=F32)
                acc = pv[:, :HEAD_DIM]
                l = pv[:, HEAD_DIM:]
                quarter = ATTN_TILE // 4
                if d == 1:
                    rows = pl.ds(pl.multiple_of(j * BLOCK, BLOCK), BLOCK)
                elif d == 4:
                    rows = pl.ds(pl.multiple_of(r * quarter + j * BLOCK, BLOCK), BLOCK)
                else:
                    rows = pl.ds((r % 4) * quarter + r // 4, BLOCK, stride=4)
                oscr[g, rows, :] = acc
                mscr[g, rows, :] = jnp.broadcast_to(m, (BLOCK, HEAD_DIM))
                lscr[g, rows, :] = l
                return carry

            lax.fori_loop(0, d * nj, unit, 0, unroll=ATTN_UNROLL)

        chunk = 2 * BLOCK
        quarter = ATTN_TILE // 4

        def merge(idx, carry):
            r4 = idx // (quarter // chunk)
            c = idx % (quarter // chunk)
            tok = pl.ds(r4 + 4 * c * chunk, chunk, stride=4)
            sub = pl.ds(pl.multiple_of(r4 * quarter + c * chunk, chunk), chunk)
            rows = [tok if d == 1 else sub for d in DILATIONS]
            ms = [mscr[g, rows[g], :] for g in range(n_g)]
            mx = functools.reduce(jnp.maximum, ms)
            es = [jnp.exp(m - mx) for m in ms]
            num = sum(e * oscr[g, rows[g], :] for g, e in enumerate(es))
            den = sum(e * lscr[g, rows[g], :] for g, e in enumerate(es))
            mrg[tok, :] = num * (1.0 / den)
            return carry

        lax.fori_loop(0, ATTN_TILE // chunk, merge, 0)
        o_ref[:, h * HEAD_DIM:(h + 1) * HEAD_DIM] = mrg[...].astype(BF16)


def _attention(qs, ks, vs, bias, batch, seq, n_heads):
    n_g = len(DILATIONS)
    hb = ATTN_HEADS if n_heads % ATTN_HEADS == 0 else 1
    in_specs, args = [], []
    for g, d in enumerate(DILATIONS):
        n = ATTN_TILE // d
        in_specs.append(pl.BlockSpec((None, hb, d, n, HEAD_DIM), lambda b, h, t: (b, h, 0, t, 0)))
        args.append(qs[g])
    for src in (ks, vs):
        for g, d in enumerate(DILATIONS):
            n = ATTN_TILE // d
            nb = n // BLOCK
            in_specs.append(pl.BlockSpec((None, hb, d, n, HEAD_DIM), lambda b, h, t: (b, h, 0, t, 0)))
            in_specs.append(pl.BlockSpec((None, hb, d, BLOCK, HEAD_DIM),
                                         lambda b, h, t, nb=nb: (b, h, 0, jnp.maximum(t * nb - 1, 0), 0)))
            args += [src[g], src[g]]
    in_specs.append(pl.BlockSpec((n_g, hb, BLOCK, 2 * BLOCK), lambda b, h, t: (0, h, 0, 0)))
    args.append(bias)
    scratch = [pltpu.VMEM((d, BLOCK + ATTN_TILE // d, HEAD_DIM), BF16) for d in DILATIONS]
    scratch += [pltpu.VMEM((d, BLOCK + ATTN_TILE // d, 2 * HEAD_DIM), BF16) for d in DILATIONS]
    scratch += [pltpu.VMEM((2, n_g, BLOCK, 2 * BLOCK), F32)]
    scratch += [pltpu.VMEM((n_g, ATTN_TILE, HEAD_DIM), F32)] * 3
    scratch += [pltpu.VMEM((ATTN_TILE, HEAD_DIM), F32)]
    return pl.pallas_call(
        functools.partial(_attn_body, hb=hb),
        out_shape=jax.ShapeDtypeStruct((batch, seq, n_heads * HEAD_DIM), BF16),
        grid=(batch, n_heads // hb, seq // ATTN_TILE),
        in_specs=in_specs,
        out_specs=pl.BlockSpec((None, ATTN_TILE, hb * HEAD_DIM), lambda b, h, t: (b, t, h)),
        scratch_shapes=scratch,
        compiler_params=_params(3),
        name="dilated_attn",
    )(*args)


def _gather_body(nrows_ref, idx_ref, idxn_ref, src_ref, o_ref, buf, sem, *, rb, nseg):
    i = pl.program_id(0)
    n = pl.num_programs(0)
    holds_rows = lambda blk: blk * rb < nrows_ref[0]

    def start_block(ids_ref, slot):
        def issue(r, c):
            pltpu.make_async_copy(src_ref.at[pl.ds(ids_ref[0, r] * nseg, nseg)],
                                  buf.at[slot, pl.ds(r * nseg, nseg)], sem.at[slot]).start()
            return c
        lax.fori_loop(0, rb, issue, 0, unroll=8)

    @pl.when((i == 0) & holds_rows(0))
    def _():
        start_block(idx_ref, 0)

    @pl.when((i + 1 < n) & holds_rows(i + 1))
    def _():
        start_block(idxn_ref, (i + 1) % 2)

    half = nseg * V7X_LANES
    for slot in range(2):
        @pl.when(holds_rows(i) & (i % 2 == slot))
        def _(slot=slot):
            pltpu.make_async_copy(src_ref.at[pl.ds(0, rb * nseg)], buf.at[slot], sem.at[slot]).wait()
            for a in range(nseg):
                u = buf[slot, pl.ds(a, rb, stride=nseg), :]
                o_ref[:, a * V7X_LANES:(a + 1) * V7X_LANES] = pltpu.bitcast(u << 16, F32).astype(BF16)
                o_ref[:, half + a * V7X_LANES:half + (a + 1) * V7X_LANES] = (
                    pltpu.bitcast(u & jnp.uint32(0xFFFF0000), F32).astype(BF16))

    @pl.when(jnp.logical_not(holds_rows(i)))
    def _():
        o_ref[...] = jnp.zeros_like(o_ref)


def _gather_rows(n_rows_valid, src_idx, hp, d, rb):
    n_rows = src_idx.shape[0]
    nseg = d // 2 // V7X_LANES
    nblk = n_rows // rb
    idx3 = src_idx.reshape(nblk, 1, rb)
    gs = pltpu.PrefetchScalarGridSpec(
        num_scalar_prefetch=1,
        grid=(nblk,),
        in_specs=[pl.BlockSpec((None, 1, rb), lambda i, nr: (i, 0, 0), memory_space=pltpu.SMEM),
                  pl.BlockSpec((None, 1, rb), lambda i, nr: (jnp.minimum(i + 1, nblk - 1), 0, 0),
                               memory_space=pltpu.SMEM),
                  pl.BlockSpec(memory_space=pl.ANY)],
        out_specs=pl.BlockSpec((rb, d), lambda i, nr: (i, 0)),
        scratch_shapes=[pltpu.VMEM((2, rb * nseg, V7X_LANES), jnp.uint32), pltpu.SemaphoreType.DMA((2,))],
    )
    return pl.pallas_call(
        functools.partial(_gather_body, rb=rb, nseg=nseg),
        out_shape=jax.ShapeDtypeStruct((n_rows, d), BF16),
        grid_spec=gs,
        compiler_params=_params(1),
        name="moe_gather",
    )(n_rows_valid, idx3, idx3, hp)


def _combine_body(pos_ref, posn_ref, y_ref, x_ref, route_ref, mod_ref, o_ref, buf, sem, *, tc, gate_row):
    i = pl.program_id(0)
    n = pl.num_programs(0)

    def start_block(p_ref, slot):
        def issue(r, c):
            pltpu.make_async_copy(y_ref.at[pl.ds(p_ref[0, r], 1)], buf.at[slot, 0, pl.ds(r, 1)], sem.at[slot]).start()
            pltpu.make_async_copy(y_ref.at[pl.ds(p_ref[0, tc + r], 1)], buf.at[slot, 1, pl.ds(r, 1)],
                                  sem.at[slot]).start()
            return c
        lax.fori_loop(0, tc, issue, 0, unroll=8)

    @pl.when(i == 0)
    def _():
        start_block(pos_ref, 0)

    @pl.when(i + 1 < n)
    def _():
        start_block(posn_ref, (i + 1) % 2)

    g1 = route_ref[:, 2:3]
    g2 = route_ref[:, 3:4]
    for slot in range(2):
        @pl.when(i % 2 == slot)
        def _(slot=slot):
            pltpu.make_async_copy(y_ref.at[pl.ds(0, tc)], buf.at[slot, 0], sem.at[slot]).wait()
            pltpu.make_async_copy(y_ref.at[pl.ds(0, tc)], buf.at[slot, 1], sem.at[slot]).wait()
            y = g1 * buf[slot, 0] + g2 * buf[slot, 1]
            o_ref[...] = x_ref[...] + mod_ref[gate_row:gate_row + 1, :] * y


def _combine(pos, y, x, route, mod, gate_row):
    m, d = x.shape
    b = mod.shape[0]
    s = m // b
    tc = pos.shape[2] // 2
    nblk = m // tc
    return pl.pallas_call(
        functools.partial(_combine_body, tc=tc, gate_row=gate_row),
        out_shape=jax.ShapeDtypeStruct((m, d), F32),
        grid=(nblk,),
        in_specs=[pl.BlockSpec((None, 1, 2 * tc), lambda i: (i, 0, 0), memory_space=pltpu.SMEM),
                  pl.BlockSpec((None, 1, 2 * tc), lambda i: (jnp.minimum(i + 1, nblk - 1), 0, 0),
                               memory_space=pltpu.SMEM),
                  pl.BlockSpec(memory_space=pl.ANY),
                  pl.BlockSpec((tc, d), lambda i: (i, 0)),
                  pl.BlockSpec((tc, V7X_LANES), lambda i: (i, 0)),
                  pl.BlockSpec((None, 8, d), lambda i: ((i * tc) // s, 0, 0))],
        out_specs=pl.BlockSpec((tc, d), lambda i: (i, 0)),
        scratch_shapes=[pltpu.VMEM((2, 2, tc, d), F32), pltpu.SemaphoreType.DMA((2,))],
        compiler_params=_params(1),
        name="moe_combine",
    )(pos, pos, y, x, route, mod)


def _rows8(*rows):
    b, d = rows[0].shape
    pad = [jnp.zeros((b, d), F32)] * (8 - len(rows))
    return jnp.stack(list(rows) + pad, axis=1)


def _par8(*rows):
    d = rows[0].shape[0]
    pad = [jnp.zeros((d,), F32)] * (8 - len(rows))
    return jnp.stack(list(rows) + pad, axis=0)


def kernel(x, c, ada_w, ada_b, norm_g, pool_w, pool_scale, kv_ada_w, kv_ada_b, kv_norm_g, w_k, w_v, k_norm_g,
           w_q, q_norm_g, w_o, rel_bias, ffn_w1, ffn_w3, ffn_w2, router_w, moe_w1, moe_w3, moe_w2):
    batch, seq, d = x.shape
    n_tok = batch * seq
    n_heads = d // HEAD_DIM
    n_exp = router_w.shape[2]
    assert d % HEAD_DIM == 0 and seq % ATTN_TILE == 0 and batch <= 8

    c8 = jnp.pad(c, ((0, 8 - batch), (0, 0)))
    mods = _ada(c8, ada_w.reshape(4, d, 3 * d), ada_b.reshape(4, 1, 3 * d))[:, :batch]
    kvm = _ada(c8, kv_ada_w[None], kv_ada_b[None, None])[0, :batch]
    sh = lambda m: (mods[m, :, :d], mods[m, :, d:2 * d], mods[m, :, 2 * d:])
    (s00, c00, g00), (s01, c01, g01), (s10, c10, g10), (s11, c11, g11) = sh(0), sh(1), sh(2), sh(3)
    kv_shift, kv_scale = kvm[:, :d], kvm[:, d:]

    x1, h2 = _mixer(x, _rows8(s00, c00, g00, s01, c01), _par8(norm_g[0, 0], norm_g[0, 1], pool_scale[0]),
                    pool_w[0].astype(BF16))
    tm_ffn = _pick(n_tok, 2048, 16)
    gact = _swiglu_up(jnp.zeros((n_tok // tm_ffn,), jnp.int32), jnp.full((1,), n_tok // tm_ffn, jnp.int32),
                      h2.reshape(n_tok, d), ffn_w1[:1], ffn_w3[:1], tm_ffn, 256, "ffn_up")
    x2 = _matmul_residual(gact, ffn_w2[0].astype(BF16), x1.reshape(n_tok, d), _rows8(g01), 0, 512, 512, "ffn_down")

    kvh, h3 = _norm2(x2.reshape(batch, seq, d), _rows8(kv_shift, kv_scale, s10, c10), _par8(kv_norm_g, norm_g[1, 0]))
    kvh, h3 = kvh.reshape(n_tok, d), h3.reshape(n_tok, d)
    ks = _proj_heads(kvh, w_k, 0, k_norm_g.reshape(1, HEAD_DIM), DILATIONS, True, 1.0, batch, "proj_k")
    vs = _proj_heads(kvh, w_v, 0, k_norm_g.reshape(1, HEAD_DIM), DILATIONS, False, 1.0, batch, "proj_v")
    qs = [_proj_heads(h3, w_q[0], g * d, q_norm_g[0].reshape(1, HEAD_DIM), (dl,), True, HEAD_DIM ** -0.5, batch,
                      f"proj_q{dl}")[0] for g, dl in enumerate(DILATIONS)]

    bias = _bias_bands(rel_bias, n_heads)
    att = _attention(qs, ks, vs, bias, batch, seq, n_heads)
    x3 = _matmul_residual(att.reshape(n_tok, d), w_o[0].astype(BF16), x2, _rows8(g10), 0, 1024, 1024, "attn_out")

    rw_pad = jnp.pad(router_w[0], ((0, 0), (0, V7X_LANES - n_exp)))
    hp, route = _route(x3.reshape(batch, seq, d), _rows8(s11, c11), _par8(norm_g[1, 1]), rw_pad, n_exp)
    route = route.reshape(n_tok, V7X_LANES)

    tm = _pick(n_tok, 512, 256)
    r_pad = TOP_K * n_tok + n_exp * tm
    e_flat = jnp.concatenate([route[:, 0], route[:, 1]]).astype(jnp.int32)
    onehot = (e_flat[:, None] == jnp.arange(n_exp, dtype=jnp.int32)[None, :]).astype(jnp.int32)
    csum = jnp.cumsum(onehot, axis=0)
    rank = jnp.sum((csum - 1) * onehot, axis=1)
    padded = ((csum[-1] + tm - 1) // tm) * tm
    gend = jnp.cumsum(padded)
    dest = (gend - padded)[e_flat] + rank
    tok = jnp.arange(TOP_K * n_tok, dtype=jnp.int32) % n_tok
    src_idx = jnp.zeros((r_pad,), jnp.int32).at[dest].set(tok)
    tile_expert = jnp.minimum(
        jnp.searchsorted(gend, jnp.arange(r_pad // tm, dtype=jnp.int32) * tm, side="right"), n_exp - 1
    ).astype(jnp.int32)

    n_valid = (gend[-1:] // tm).astype(jnp.int32)

    xg = _gather_rows(gend[-1:].astype(jnp.int32), src_idx, hp, d, tm)
    hg = _swiglu_up(tile_expert, n_valid, xg, moe_w1[0], moe_w3[0], tm, 512, "moe_up")
    yg = _down_grouped(tile_expert, n_valid, hg, moe_w2[0], tm)

    tc = _pick(n_tok, 256, 8)
    pos = jnp.concatenate([dest[:n_tok].reshape(n_tok // tc, 1, tc), dest[n_tok:].reshape(n_tok // tc, 1, tc)], axis=2)
    out = _combine(pos.astype(jnp.int32), yg, x3, route, _rows8(g11), 0)
    return out.reshape(batch, seq, d)
```
